```python
import jax
import jax.numpy as jnp
from jax import lax
import numpy as np

D_MODEL = 2048
BATCH = 1
SEQ = 8192
DEPTH = 4

N_MIXERS = 4
HEADS_PER_GROUP = 4
GROUP_WIDTH = D_MODEL // N_MIXERS
HEAD_DIM = GROUP_WIDTH // HEADS_PER_GROUP
MIX_WIDTH = N_MIXERS * GROUP_WIDTH
ROPE_THETA = 10000.0
Q_BLOCK = 128
RET_CHUNK = 128
HGRN_CHUNK = 64
NSA_CMP_LEN = 32
NSA_CMP_STRIDE = 16
NSA_SLC_LEN = 64
NSA_TOPK = 16
NSA_WINDOW = 512
N_MEM = 256
XATTN_HEADS = 4
XATTN_DIM = 128
D_FF = 5632
CONV_WIDTH = 3
NORM_EPS = 1e-6
MASK_VALUE = -1e30
FORCED_SCORE = 1e4
MIN_FORGET = 1e-6
IN_COLS = 12 * GROUP_WIDTH + 6 * HEAD_DIM + 3 * HEADS_PER_GROUP

kernel_name = 'hybrid_parallel_head_groups_decoder'


def rms_norm(x, g):
    xf = x.astype(jnp.float32)
    y = xf * lax.rsqrt(jnp.mean(xf * xf, axis=-1, keepdims=True) + NORM_EPS)
    return (y * g.astype(jnp.float32)).astype(x.dtype)


def head_norm(o, g, center):
    of = o.astype(jnp.float32)
    if center:
        of = of - jnp.mean(of, axis=-1, keepdims=True)
    y = of * lax.rsqrt(jnp.mean(of * of, axis=-1, keepdims=True) + NORM_EPS)
    return y * g.astype(jnp.float32).reshape(o.shape[-2], o.shape[-1])


def rope(x, pos):
    half = x.shape[-1] // 2
    inv = ROPE_THETA ** (-jnp.arange(half, dtype=jnp.float32) / half)
    ang = pos.astype(jnp.float32)[:, :, None, None] * inv
    cos, sin = jnp.cos(ang), jnp.sin(ang)
    x1 = x[..., :half].astype(jnp.float32)
    x2 = x[..., half:].astype(jnp.float32)
    return jnp.concatenate([x1 * cos - x2 * sin, x2 * cos + x1 * sin], axis=-1).astype(x.dtype)


def masked_softmax(s, mask):
    s = jnp.where(mask, s, MASK_VALUE)
    m = jnp.max(s, axis=-1, keepdims=True)
    p = jnp.exp(s - m) * mask
    return p / jnp.maximum(jnp.sum(p, axis=-1, keepdims=True), 1e-30)


def retention(q, k, v):
    B, S, H, D = q.shape
    C = RET_CHUNK
    nc = S // C
    f32 = jnp.float32
    log_gamma = jnp.log1p(-jnp.exp2(-5.0 - jnp.arange(H, dtype=f32)))
    qc = q.astype(f32).reshape(B, nc, C, H, D)
    kc = k.astype(f32).reshape(B, nc, C, H, D) * D ** -0.5
    vc = v.astype(f32).reshape(B, nc, C, H, D)
    pos = jnp.arange(C, dtype=f32)
    rel = pos[:, None] - pos[None, :]
    causal = rel >= 0
    decay = jnp.where(causal, jnp.exp(log_gamma[:, None, None] * jnp.where(causal, rel, 0.0)), 0.0)
    scores = jnp.einsum('bcnhd,bcmhd->bchnm', qc, kc) * decay
    o_intra = jnp.einsum('bchnm,bcmhe->bcnhe', scores, vc)
    k_w = jnp.exp(log_gamma[:, None] * (C - 1.0 - pos))
    kv = jnp.einsum('bcmhd,hm,bcmhe->cbhde', kc, k_w, vc)
    chunk_decay = jnp.exp(log_gamma * C)[None, :, None, None]

    def step(state, kv_c):
        return state * chunk_decay + kv_c, state

    _, prev = lax.scan(step, jnp.zeros((B, H, D, D), f32), kv)
    q_w = jnp.exp(log_gamma[:, None] * (pos + 1.0))
    o_inter = jnp.einsum('bcnhd,cbhde,hn->bcnhe', qc, prev, q_w)
    return (o_intra + o_inter).reshape(B, S, H, D)


def stick_breaking(q, k, v):
    B, S, H, D = q.shape
    nb = S // Q_BLOCK
    kh = k.transpose(0, 2, 1, 3)
    vh = v.transpose(0, 2, 1, 3)
    qb = q.reshape(B, nb, Q_BLOCK, H, D).transpose(1, 0, 3, 2, 4)
    key_pos = jnp.arange(S)
    scale = D ** -0.5

    def block(args):
        qi, bi = args
        z = jnp.einsum('bhqd,bhkd->bhqk', qi, kh).astype(jnp.float32) * scale
        q_pos = bi * Q_BLOCK + jnp.arange(Q_BLOCK)
        strict = key_pos[None, :] < q_pos[:, None]
        log_1m = jnp.where(strict, jax.nn.log_sigmoid(-z), 0.0)
        between = lax.cumsum(log_1m, axis=3, reverse=True) - log_1m
        w = jnp.where(strict, jnp.exp(jax.nn.log_sigmoid(z) + between), 0.0)
        return jnp.einsum('bhqk,bhkd->bhqd', w.astype(vh.dtype), vh)

    o = lax.map(block, (qb, jnp.arange(nb)))
    return o.transpose(1, 0, 3, 2, 4).reshape(B, S, H, D)


def hgrn2(q, f_logit, inp, lower_bound):
    B, S, H, D = q.shape
    C = HGRN_CHUNK
    nc = S // C
    f32 = jnp.float32
    lb = lower_bound.astype(f32)
    f = lb + (1.0 - lb) * jax.nn.sigmoid(f_logit.astype(f32))
    log_f = jnp.log(jnp.maximum(f, MIN_FORGET))
    k = 1.0 - f

    def chunks(t):
        return t.astype(f32).reshape(B, nc, C, H, D).transpose(1, 0, 3, 2, 4)

    tril = jnp.tril(jnp.ones((C, C), dtype=bool))[..., None]

    def step(state, xs):
        qi, ki, vi, lfi = xs
        G = jnp.cumsum(lfi, axis=2)
        diff = G[:, :, :, None, :] - G[:, :, None, :, :]
        decay = jnp.where(tril, jnp.exp(jnp.where(tril, diff, 0.0)), 0.0)
        scores = jnp.einsum('bhnd,bhmd,bhnmd->bhnm', qi, ki, decay)
        o = (jnp.einsum('bhnm,bhme->bhne', scores, vi)
             + jnp.einsum('bhnd,bhde->bhne', qi * jnp.exp(G), state))
        g_last = G[:, :, -1, :]
        new_state = (jnp.exp(g_last)[..., None] * state
                     + jnp.einsum('bhmd,bhme->bhde', ki * jnp.exp(g_last[:, :, None, :] - G), vi))
        return new_state, o

    _, o = lax.scan(step, jnp.zeros((B, H, D, D), f32),
                    (chunks(q) * D ** -0.5, chunks(k), chunks(inp), chunks(log_f)))
    return o.transpose(1, 0, 3, 2, 4).reshape(B, S, H, D)


def nsa(q, kc, vc, ks, vs, kw, vw, gates, pos_k, pos_v, w_ck, w_cv):
    B, S, H, D = q.shape
    nb = S // Q_BLOCK
    n_cmp = (S - NSA_CMP_LEN) // NSA_CMP_STRIDE + 1
    n_slc = S // NSA_SLC_LEN
    top_k = min(NSA_TOPK, n_slc)
    scale = D ** -0.5
    f32 = jnp.float32
    cmp_start = np.arange(n_cmp) * NSA_CMP_STRIDE
    win_idx = cmp_start[:, None] + np.arange(NSA_CMP_LEN)[None, :]
    k_cmp = jnp.einsum('bnld,lde->bne', kc[:, win_idx] + pos_k, w_ck)
    v_cmp = jnp.einsum('bnld,lde->bne', vc[:, win_idx] + pos_v, w_cv)
    cmp_last = jnp.asarray(cmp_start + NSA_CMP_LEN - 1, dtype=jnp.int32)
    slc_start = np.arange(n_slc) * NSA_SLC_LEN
    ov = (np.minimum(cmp_start[:, None] + NSA_CMP_LEN, slc_start[None, :] + NSA_SLC_LEN)
          - np.maximum(cmp_start[:, None], slc_start[None, :]))
    overlap = jnp.asarray(np.clip(ov, 0, None) / NSA_CMP_LEN, dtype=f32)
    ks_blocks = ks.reshape(B, n_slc, NSA_SLC_LEN, D)
    vs_blocks = vs.reshape(B, n_slc, NSA_SLC_LEN, D)
    kw_pad = jnp.pad(kw, ((0, 0), (NSA_WINDOW, 0), (0, 0)))
    vw_pad = jnp.pad(vw, ((0, 0), (NSA_WINDOW, 0), (0, 0)))
    qb = q.reshape(B, nb, Q_BLOCK, H, D).transpose(1, 0, 3, 2, 4)
    gb = jax.nn.sigmoid(gates.astype(f32)).reshape(B, nb, Q_BLOCK, H, 3).transpose(1, 0, 3, 2, 4)
    gather = jax.vmap(lambda blocks, idx: blocks[idx])
    blk = jnp.arange(n_slc)

    def block(args):
        qi, gi, bi = args
        q_pos = bi * Q_BLOCK + jnp.arange(Q_BLOCK)
        s_cmp = jnp.einsum('bhqd,bnd->bhqn', qi, k_cmp).astype(f32) * scale
        p_cmp = masked_softmax(s_cmp, cmp_last[None, :] <= q_pos[:, None])
        o_cmp = jnp.einsum('bhqn,bnd->bhqd', p_cmp, v_cmp)
        imp = jnp.einsum('bhqn,nj->bqj', p_cmp, overlap)
        cur = q_pos // NSA_SLC_LEN
        forced = (blk[None, :] == 0) | (blk[None, :] == cur[:, None]) | (blk[None, :] == cur[:, None] - 1)
        future = blk[None, :] * NSA_SLC_LEN > q_pos[:, None]
        imp = jnp.where(future, -1.0, jnp.where(forced, FORCED_SCORE, imp))
        _, sel = lax.top_k(imp, top_k)
        k_sel = gather(ks_blocks, sel).reshape(B, Q_BLOCK, top_k * NSA_SLC_LEN, D)
        v_sel = gather(vs_blocks, sel).reshape(B, Q_BLOCK, top_k * NSA_SLC_LEN, D)
        tok_pos = (sel[..., None] * NSA_SLC_LEN + jnp.arange(NSA_SLC_LEN)).reshape(B, Q_BLOCK, top_k * NSA_SLC_LEN)
        s_slc = jnp.einsum('bhqd,bqnd->bhqn', qi, k_sel).astype(f32) * scale
        p_slc = masked_softmax(s_slc, (tok_pos <= q_pos[None, :, None])[:, None])
        o_slc = jnp.einsum('bhqn,bqnd->bhqd', p_slc, v_sel)
        k_win = lax.dynamic_slice_in_dim(kw_pad, bi * Q_BLOCK, NSA_WINDOW + Q_BLOCK, axis=1)
        v_win = lax.dynamic_slice_in_dim(vw_pad, bi * Q_BLOCK, NSA_WINDOW + Q_BLOCK, axis=1)
        w_pos = bi * Q_BLOCK - NSA_WINDOW + jnp.arange(NSA_WINDOW + Q_BLOCK)
        m_win = ((w_pos[None, :] >= 0) & (w_pos[None, :] <= q_pos[:, None])
                 & (q_pos[:, None] - w_pos[None, :] < NSA_WINDOW))
        s_win = jnp.einsum('bhqd,bkd->bhqk', qi, k_win).astype(f32) * scale
        o_win = jnp.einsum('bhqk,bkd->bhqd', masked_softmax(s_win, m_win), v_win)
        return gi[..., 0:1] * o_cmp + gi[..., 1:2] * o_slc + gi[..., 2:3] * o_win

    o = lax.map(block, (qb, gb, jnp.arange(nb)))
    return o.transpose(1, 0, 3, 2, 4).reshape(B, S, H, D)


def cross_attention(h, mem_n, wq, wk, wv, wo):
    B, S, _ = h.shape
    M = mem_n.shape[1]
    q = (h @ wq).reshape(B, S, XATTN_HEADS, XATTN_DIM)
    k = (mem_n @ wk).reshape(B, M, XATTN_HEADS, XATTN_DIM)
    v = (mem_n @ wv).reshape(B, M, XATTN_HEADS, XATTN_DIM)
    s = jnp.einsum('bshd,bmhd->bhsm', q, k).astype(jnp.float32) * XATTN_DIM ** -0.5
    p = jax.nn.softmax(s, axis=-1)
    o = jnp.einsum('bhsm,bmhd->bshd', p.astype(v.dtype), v).reshape(B, S, XATTN_HEADS * XATTN_DIM)
    return (o @ wo).astype(h.dtype)


def conv_ffn(h, w_up, conv_w, conv_b, w_down):
    u = h @ w_up
    S = u.shape[1]
    up = jnp.pad(u, ((0, 0), (CONV_WIDTH - 1, 0), (0, 0)))
    c = conv_b
    for j in range(CONV_WIDTH):
        c = c + conv_w[j] * up[:, j:j + S]
    gate, val = jnp.split(c, 2, axis=-1)
    return ((jax.nn.silu(gate) * val) @ w_down).astype(h.dtype)


def setup_inputs(seed: int = 0) -> dict:
    key = jax.random.key(seed)
    k = jax.random.split(key, 26)
    f32 = jnp.float32

    def normal(kk, shape, scale):
        return jax.random.normal(kk, shape, f32) * scale

    def gain(kk, shape):
        return 1.0 + 0.02 * jax.random.normal(kk, shape, f32)

    L = DEPTH
    XW = XATTN_HEADS * XATTN_DIM
    offset = jax.random.randint(k[2], (BATCH, 1), 0, 1024, dtype=jnp.int32)
    return {
        'x': normal(k[0], (BATCH, SEQ, D_MODEL), 1.0),
        'mem': normal(k[1], (BATCH, N_MEM, D_MODEL), 1.0),
        'positions': offset + jnp.arange(SEQ, dtype=jnp.int32)[None, :],
        'mix_norm': gain(k[3], (L, D_MODEL)),
        'w_in': normal(k[4], (L, D_MODEL, IN_COLS), D_MODEL ** -0.5),
        'ret_norm': gain(k[5], (L, GROUP_WIDTH)),
        'hgrn_lb_logits': normal(k[6], (L, GROUP_WIDTH), 0.5),
        'hgrn_norm': gain(k[7], (L, GROUP_WIDTH)),
        'nsa_pos_k': normal(k[8], (L, NSA_CMP_LEN, HEAD_DIM), 0.1),
        'nsa_pos_v': normal(k[9], (L, NSA_CMP_LEN, HEAD_DIM), 0.1),
        'nsa_w_ck': normal(k[10], (L, NSA_CMP_LEN, HEAD_DIM, HEAD_DIM), (NSA_CMP_LEN * HEAD_DIM) ** -0.5),
        'nsa_w_cv': normal(k[11], (L, NSA_CMP_LEN, HEAD_DIM, HEAD_DIM), (NSA_CMP_LEN * HEAD_DIM) ** -0.5),
        'w_out': normal(k[12], (L, MIX_WIDTH, D_MODEL), MIX_WIDTH ** -0.5),
        'xattn_norm': gain(k[13], (L, D_MODEL)),
        'mem_norm': gain(k[14], (L, D_MODEL)),
        'xattn_wq': normal(k[15], (L, D_MODEL, XW), D_MODEL ** -0.5),
        'xattn_wk': normal(k[16], (L, D_MODEL, XW), D_MODEL ** -0.5),
        'xattn_wv': normal(k[17], (L, D_MODEL, XW), D_MODEL ** -0.5),
        'xattn_wo': normal(k[18], (L, XW, D_MODEL), XW ** -0.5),
        'ffn_norm': gain(k[19], (L, D_MODEL)),
        'ffn_w_up': normal(k[20], (L, D_MODEL, 2 * D_FF), D_MODEL ** -0.5),
        'ffn_conv_w': normal(k[21], (L, CONV_WIDTH, 2 * D_FF), CONV_WIDTH ** -0.5),
        'ffn_conv_b': normal(k[22], (L, 2 * D_FF), 0.01),
        'ffn_w_down': normal(k[23], (L, D_FF, D_MODEL), D_FF ** -0.5),
        'final_norm': gain(k[24], (D_MODEL,)),
    }


def reference(x, mem, positions, mix_norm, w_in, ret_norm, hgrn_lb_logits, hgrn_norm,
              nsa_pos_k, nsa_pos_v, nsa_w_ck, nsa_w_cv, w_out, xattn_norm, mem_norm,
              xattn_wq, xattn_wk, xattn_wv, xattn_wo, ffn_norm, ffn_w_up, ffn_conv_w,
              ffn_conv_b, ffn_w_down, final_norm):
    B, S, _ = x.shape
    H, D = HEADS_PER_GROUP, HEAD_DIM
    lb_p = jax.nn.softmax(hgrn_lb_logits.astype(jnp.float32), axis=0)
    lower_bounds = jnp.cumsum(lb_p, axis=0) - lb_p[0]
    widths = [GROUP_WIDTH] * 12 + [HEAD_DIM] * 6
    split_points = [int(v) for v in np.cumsum(widths)]

    def heads(t):
        return t.reshape(B, S, H, D)

    def rope_kv(t):
        return rope(t[:, :, None, :], positions)[:, :, 0, :]

    for layer in range(DEPTH):
        h = rms_norm(x, mix_norm[layer])
        proj = h @ w_in[layer]
        (rq, rk, rv, rg, sq, sk, sv, gq, gf, gi, gg, nq,
         kc, vc, ks, vs, kw, vw, ng) = jnp.split(proj, split_points, axis=-1)
        o_ret = retention(rope(heads(rq), positions), rope(heads(rk), positions), heads(rv))
        o_ret = head_norm(o_ret, ret_norm[layer], True) * jax.nn.silu(heads(rg).astype(jnp.float32))
        o_sb = stick_breaking(heads(sq), heads(sk), heads(sv))
        o_hg = hgrn2(heads(gq), heads(gf), heads(gi), lower_bounds[layer].reshape(H, D))
        o_hg = head_norm(o_hg, hgrn_norm[layer], False) * jax.nn.silu(heads(gg).astype(jnp.float32))
        o_nsa = nsa(rope(heads(nq), positions), rope_kv(kc), vc, rope_kv(ks), vs, rope_kv(kw), vw,
                    ng.reshape(B, S, H, 3), nsa_pos_k[layer], nsa_pos_v[layer],
                    nsa_w_ck[layer], nsa_w_cv[layer])
        mix = jnp.concatenate([o.reshape(B, S, GROUP_WIDTH).astype(x.dtype)
                               for o in (o_ret, o_sb, o_hg, o_nsa)], axis=-1)
        x = x + (mix @ w_out[layer]).astype(x.dtype)
        x = x + cross_attention(rms_norm(x, xattn_norm[layer]), rms_norm(mem, mem_norm[layer]),
                                xattn_wq[layer], xattn_wk[layer], xattn_wv[layer], xattn_wo[layer])
        x = x + conv_ffn(rms_norm(x, ffn_norm[layer]), ffn_w_up[layer], ffn_conv_w[layer],
                         ffn_conv_b[layer], ffn_w_down[layer])
    return rms_norm(x, final_norm)
```

```python
import functools
import math

import numpy as np
import jax
import jax.numpy as jnp
from jax import lax
from jax.experimental import pallas as pl
from jax.experimental.pallas import tpu as pltpu

F32 = jnp.float32
BF16 = jnp.bfloat16

D_MODEL = 2048
DEPTH = 4
N_HEADS = 4
HEAD_DIM = 128
GROUP_WIDTH = N_HEADS * HEAD_DIM
ROPE_THETA = 10000.0
RET_CHUNK = 128
HGRN_CHUNK = 64
NSA_CMP_LEN = 32
NSA_CMP_STRIDE = 16
NSA_SLC_LEN = 64
NSA_SLC_SHIFT = 6
NSA_TOPK = 16
NSA_WINDOW = 512
N_MEM = 256
D_FF = 5632
NORM_EPS = 1e-6
MASK_VALUE = -1e30
FORCED_SCORE = 1e4
MIN_FORGET = 1e-6
IN_COLS = 12 * GROUP_WIDTH + 6 * HEAD_DIM + 3 * N_HEADS
IN_COLS_PAD = 7168
ATTN_SCALE = HEAD_DIM ** -0.5

COL_RQ, COL_RK, COL_RV, COL_RG = 0, 4, 8, 12
COL_SQ, COL_SK, COL_SV = 16, 20, 24
COL_GQ, COL_GF, COL_GI, COL_GG = 28, 32, 36, 40
COL_NQ = 44
COL_KC, COL_VC, COL_KS, COL_VS, COL_KW, COL_VW, COL_NG = 48, 49, 50, 51, 52, 53, 54

V7X_VMEM_BYTES = 64 * 1024 * 1024
LANES = 128


def _cparams(semantics, vmem_mb):
    return pltpu.CompilerParams(dimension_semantics=semantics,
                                vmem_limit_bytes=min(vmem_mb * 1024 * 1024, V7X_VMEM_BYTES - (8 << 20)))


def _dot(a, b):
    return jnp.dot(a, b, preferred_element_type=F32)


def _dot_nt(a, b):
    return lax.dot_general(a, b, (((1,), (1,)), ((), ())), preferred_element_type=F32)


def _dot_tn(a, b):
    return lax.dot_general(a, b, (((0,), (0,)), ((), ())), preferred_element_type=F32)


def _split3(x):
    h1 = x.astype(BF16)
    r1 = x - h1.astype(F32)
    h2 = r1.astype(BF16)
    r2 = r1 - h2.astype(F32)
    return h1, h2, r2.astype(BF16)


def _dot_exact_lhs(a_bf16, x_f32):
    h1, h2, h3 = _split3(x_f32)
    return _dot(a_bf16, h1) + _dot(a_bf16, h2) + _dot(a_bf16, h3)


def _rms(x, gain):
    ms = jnp.mean(x * x, axis=-1, keepdims=True)
    return x * lax.rsqrt(ms + NORM_EPS) * gain


def _rope(x, cos, sin_signed):
    return x * cos + pltpu.roll(x, HEAD_DIM // 2, axis=1) * sin_signed


def _silu(x):
    return x * jax.nn.sigmoid(x)


def _rope_table_kernel(pos_ref, inv_ref, cos_ref, sin_ref):
    ang = pos_ref[...].astype(F32) * inv_ref[...]
    lane = lax.broadcasted_iota(jnp.int32, ang.shape, 1)
    cos_ref[...] = jnp.cos(ang)
    s = jnp.sin(ang)
    sin_ref[...] = jnp.where(lane < HEAD_DIM // 2, -s, s)


def _rope_tables(positions, S):
    half = HEAD_DIM // 2
    inv = ROPE_THETA ** (-jnp.arange(half, dtype=F32) / half)
    inv = jnp.concatenate([inv, inv]).reshape(1, HEAD_DIM)
    pos = positions.reshape(S, 1)
    tm = min(S, 1024)
    return pl.pallas_call(
        _rope_table_kernel,
        grid=(S // tm,),
        in_specs=[pl.BlockSpec((tm, 1), lambda i: (i, 0)),
                  pl.BlockSpec((1, HEAD_DIM), lambda i: (0, 0))],
        out_specs=[pl.BlockSpec((tm, HEAD_DIM), lambda i: (i, 0)),
                   pl.BlockSpec((tm, HEAD_DIM), lambda i: (i, 0))],
        out_shape=[jax.ShapeDtypeStruct((S, HEAD_DIM), F32)] * 2,
        compiler_params=_cparams(("arbitrary",), 32),
    )(pos, inv)


def _inproj_kernel(x_ref, g_ref, w_ref, o32_ref, o16_ref, h_ref):
    @pl.when(pl.program_id(1) == 0)
    def _():
        h_ref[...] = _rms(x_ref[...], g_ref[...]).astype(BF16)

    y = _dot(h_ref[...], w_ref[...])
    o32_ref[...] = y
    o16_ref[...] = y.astype(BF16)


def _inproj(x, gain, w16, tm, tn):
    S, K = x.shape
    N = w16.shape[1]
    return pl.pallas_call(
        _inproj_kernel,
        grid=(S // tm, N // tn),
        in_specs=[pl.BlockSpec((tm, K), lambda i, j: (i, 0)),
                  pl.BlockSpec((1, K), lambda i, j: (0, 0)),
                  pl.BlockSpec((K, tn), lambda i, j: (0, j))],
        out_specs=[pl.BlockSpec((tm, tn), lambda i, j: (i, j)),
                   pl.BlockSpec((tm, tn), lambda i, j: (i, j))],
        out_shape=[jax.ShapeDtypeStruct((S, N), F32), jax.ShapeDtypeStruct((S, N), BF16)],
        scratch_shapes=[pltpu.VMEM((tm, K), BF16)],
        compiler_params=_cparams(("parallel", "arbitrary"), 48),
    )(x, gain.reshape(1, K), w16)


def _normproj_kernel(x_ref, g_ref, w_ref, o_ref):
    h = _rms(x_ref[...], g_ref[...]).astype(BF16)
    o_ref[...] = _dot(h, w_ref[...]).astype(o_ref.dtype)


def _normproj(x, gain, w16, out_dtype):
    M, K = x.shape
    N = w16.shape[1]
    return pl.pallas_call(
        _normproj_kernel,
        grid=(1,),
        in_specs=[pl.BlockSpec((M, K), lambda i: (0, 0)),
                  pl.BlockSpec((1, K), lambda i: (0, 0)),
                  pl.BlockSpec((K, N), lambda i: (0, 0))],
        out_specs=pl.BlockSpec((M, N), lambda i: (0, 0)),
        out_shape=jax.ShapeDtypeStruct((M, N), out_dtype),
        compiler_params=_cparams(("arbitrary",), 32),
    )(x, gain.reshape(1, K), w16)


_RET_LOG_GAMMA = [math.log1p(-(2.0 ** (-5.0 - h))) for h in range(N_HEADS)]


def _ret_kernel(q_ref, k_ref, v_ref, g_ref, cos_ref, sin_ref, gain_ref, o_ref, st_ref):
    C = RET_CHUNK

    @pl.when(pl.program_id(0) == 0)
    def _():
        st_ref[...] = jnp.zeros_like(st_ref)

    cos = cos_ref[...]
    sin = sin_ref[...]
    row = lax.broadcasted_iota(jnp.int32, (C, C), 0)
    col = lax.broadcasted_iota(jnp.int32, (C, C), 1)
    rel = (row - col).astype(F32)
    rowf = row.astype(F32)
    for h in range(N_HEADS):
        hs = slice(h * HEAD_DIM, (h + 1) * HEAD_DIM)
        lg = _RET_LOG_GAMMA[h]
        decay = jnp.where(rel >= 0, jnp.exp(lg * jnp.maximum(rel, 0.0)), 0.0)
        q = _rope(q_ref[:, hs], cos, sin)
        k = _rope(k_ref[:, hs], cos, sin) * ATTN_SCALE
        v = v_ref[:, hs]
        st = st_ref[h]
        s = _dot_nt(q.astype(BF16), k.astype(BF16)) * decay
        o = _dot(s.astype(BF16), v)
        q_w = jnp.exp(lg * (rowf + 1.0))
        o = o + _dot_nt((q * q_w).astype(BF16), st.astype(BF16))
        k_w = jnp.exp(lg * (C - 1.0 - rowf))
        st_ref[h] = st * math.exp(lg * C) + _dot_tn(v, (k * k_w).astype(BF16))
        oc = o - jnp.mean(o, axis=-1, keepdims=True)
        y = oc * lax.rsqrt(jnp.mean(oc * oc, axis=-1, keepdims=True) + NORM_EPS) * gain_ref[:, hs]
        o_ref[:, hs] = (y * _silu(g_ref[:, hs])).astype(o_ref.dtype)


def _retention(p32, p16, cos, sin, gain, S):
    C = RET_CHUNK
    gw = GROUP_WIDTH // LANES
    return pl.pallas_call(
        _ret_kernel,
        grid=(S // C,),
        in_specs=[pl.BlockSpec((C, GROUP_WIDTH), lambda c: (c, COL_RQ // gw)),
                  pl.BlockSpec((C, GROUP_WIDTH), lambda c: (c, COL_RK // gw)),
                  pl.BlockSpec((C, GROUP_WIDTH), lambda c: (c, COL_RV // gw)),
                  pl.BlockSpec((C, GROUP_WIDTH), lambda c: (c, COL_RG // gw)),
                  pl.BlockSpec((C, HEAD_DIM), lambda c: (c, 0)),
                  pl.BlockSpec((C, HEAD_DIM), lambda c: (c, 0)),
                  pl.BlockSpec((1, GROUP_WIDTH), lambda c: (0, 0))],
        out_specs=pl.BlockSpec((C, GROUP_WIDTH), lambda c: (c, 0)),
        out_shape=jax.ShapeDtypeStruct((S, GROUP_WIDTH), BF16),
        scratch_shapes=[pltpu.VMEM((N_HEADS, HEAD_DIM, HEAD_DIM), F32)],
        compiler_params=_cparams(("arbitrary",), 32),
    )(p32, p32, p16, p32, cos, sin, gain.reshape(1, GROUP_WIDTH))


SB_TILE = 256


def _sb_tile(q, k_ref, v_ref, j, run, acc, upper, diag):
    T = SB_TILE
    start = pl.multiple_of(j * T, T)
    k = k_ref[pl.ds(start, T), :]
    v = v_ref[pl.ds(start, T), :]
    z = _dot_nt(q, k) * ATTN_SCALE
    sp = jnp.maximum(z, 0.0) + jnp.log1p(jnp.exp(-jnp.abs(z)))
    log_1m = -sp
    if diag:
        row = lax.broadcasted_iota(jnp.int32, (T, T), 0)
        col = lax.broadcasted_iota(jnp.int32, (T, T), 1)
        strict = col < row
        log_1m = jnp.where(strict, log_1m, 0.0)
    between = _dot_exact_lhs_rhs(log_1m, upper) + run
    w = jnp.exp((z - sp) + between)
    if diag:
        w = jnp.where(strict, w, 0.0)
    acc = acc + _dot(w.astype(BF16), v)
    run = between[:, 0:1] + log_1m[:, 0:1]
    return run, acc


def _dot_exact_lhs_rhs(x_f32, b_bf16):
    h1, h2, h3 = _split3(x_f32)
    return _dot(h1, b_bf16) + _dot(h2, b_bf16) + _dot(h3, b_bf16)


def _sb_kernel(q_ref, k_ref, v_ref, o_ref):
    T = SB_TILE
    i = pl.program_id(1)
    q = q_ref[...]
    r = lax.broadcasted_iota(jnp.int32, (T, T), 0)
    c = lax.broadcasted_iota(jnp.int32, (T, T), 1)
    upper = jnp.where(r > c, 1.0, 0.0).astype(BF16)
    run = jnp.zeros((T, 1), F32)
    acc = jnp.zeros((T, HEAD_DIM), F32)
    run, acc = _sb_tile(q, k_ref, v_ref, i, run, acc, upper, True)

    def body(step, carry):
        run, acc = carry
        return _sb_tile(q, k_ref, v_ref, i - 1 - step, run, acc, upper, False)

    run, acc = lax.fori_loop(0, i, body, (run, acc))
    o_ref[...] = acc.astype(o_ref.dtype)


def _stick_breaking(p16, S):
    T = SB_TILE
    return pl.pallas_call(
        _sb_kernel,
        grid=(N_HEADS, S // T),
        in_specs=[pl.BlockSpec((T, HEAD_DIM), lambda h, i: (i, COL_SQ + h)),
                  pl.BlockSpec((S, HEAD_DIM), lambda h, i: (0, COL_SK + h)),
                  pl.BlockSpec((S, HEAD_DIM), lambda h, i: (0, COL_SV + h))],
        out_specs=pl.BlockSpec((T, HEAD_DIM), lambda h, i: (i, h)),
        out_shape=jax.ShapeDtypeStruct((S, GROUP_WIDTH), BF16),
        compiler_params=_cparams(("parallel", "arbitrary"), 40),
    )(p16, p16, p16)


_HG_LEVELS = (32, 16, 8, 4, 2, 1)


def _hgrn_constants():
    C = HGRN_CHUNK
    n = np.arange(C)
    seg = np.zeros((8 * C, C), np.float32)
    pair = np.zeros((7, C, C), np.float32)
    for l, s in enumerate(_HG_LEVELS):
        blk = n // s
        odd = (blk % 2) == 1
        a = blk * s
        e = a + s - 1
        j = n[None, :]
        t_odd = (j >= a[:, None]) & (j <= n[:, None])
        t_even = (j > n[:, None]) & (j <= e[:, None])
        seg[l * C:(l + 1) * C] = np.where(odd[:, None], t_odd, t_even)
        pair[l] = odd[:, None] & (blk[None, :] == blk[:, None] - 1)
    seg[6 * C:7 * C] = n[None, :] <= n[:, None]
    seg[7 * C:8 * C] = n[None, :] > n[:, None]
    pair[6] = np.eye(C)
    return seg, pair


def _hgrn_kernel(layer, q_ref, f_ref, v_ref, g_ref, lbl_ref, gain_ref, seg_ref, pair_ref, o_ref, st_ref):
    C = HGRN_CHUNK

    @pl.when(pl.program_id(0) == 0)
    def _():
        st_ref[...] = jnp.zeros_like(st_ref)

    logits = lbl_ref[...]
    e = jnp.exp(logits - jnp.max(logits, axis=0, keepdims=True))
    p = e / jnp.sum(e, axis=0, keepdims=True)
    lb = jnp.zeros((1, GROUP_WIDTH), F32)
    for l in range(1, layer + 1):
        lb = lb + p[l:l + 1, :]

    f = lb + (1.0 - lb) * jax.nn.sigmoid(f_ref[...])
    log_f = jnp.log(jnp.maximum(f, MIN_FORGET))
    kk = 1.0 - f
    ex = jnp.exp(_dot_exact_lhs(seg_ref[...], log_f))

    for h in range(N_HEADS):
        hs = slice(h * HEAD_DIM, (h + 1) * HEAD_DIM)
        q = q_ref[:, hs] * ATTN_SCALE
        k = kk[:, hs]
        v = v_ref[:, hs]
        s = pair_ref[6] * _dot_nt(q.astype(BF16), k.astype(BF16))
        for l in range(6):
            el = ex[l * C:(l + 1) * C, hs]
            s = s + pair_ref[l] * _dot_nt((q * el).astype(BF16), (k * el).astype(BF16))
        e_cum = ex[6 * C:7 * C, hs]
        e_rest = ex[7 * C:8 * C, hs]
        st = st_ref[h]
        o = _dot(s.astype(BF16), v) + _dot_nt((q * e_cum).astype(BF16), st.astype(BF16))
        st_ref[h] = st * e_cum[C - 1:C, :] + _dot_tn(v, (k * e_rest).astype(BF16))
        y = o * lax.rsqrt(jnp.mean(o * o, axis=-1, keepdims=True) + NORM_EPS) * gain_ref[:, hs]
        o_ref[:, hs] = (y * _silu(g_ref[:, hs])).astype(o_ref.dtype)


def _hgrn2(p32, p16, lb_logits, gain, layer, S):
    C = HGRN_CHUNK
    gw = GROUP_WIDTH // LANES
    seg, pair = _hgrn_constants()
    return pl.pallas_call(
        functools.partial(_hgrn_kernel, layer),
        grid=(S // C,),
        in_specs=[pl.BlockSpec((C, GROUP_WIDTH), lambda c: (c, COL_GQ // gw)),
                  pl.BlockSpec((C, GROUP_WIDTH), lambda c: (c, COL_GF // gw)),
                  pl.BlockSpec((C, GROUP_WIDTH), lambda c: (c, COL_GI // gw)),
                  pl.BlockSpec((C, GROUP_WIDTH), lambda c: (c, COL_GG // gw)),
                  pl.BlockSpec((DEPTH, GROUP_WIDTH), lambda c: (0, 0)),
                  pl.BlockSpec((1, GROUP_WIDTH), lambda c: (0, 0)),
                  pl.BlockSpec((8 * C, C), lambda c: (0, 0)),
                  pl.BlockSpec((7, C, C), lambda c: (0, 0, 0))],
        out_specs=pl.BlockSpec((C, GROUP_WIDTH), lambda c: (c, 0)),
        out_shape=jax.ShapeDtypeStruct((S, GROUP_WIDTH), BF16),
        scratch_shapes=[pltpu.VMEM((N_HEADS, HEAD_DIM, HEAD_DIM), F32)],
        compiler_params=_cparams(("arbitrary",), 32),
    )(p32, p32, p16, p32, lb_logits, gain.reshape(1, GROUP_WIDTH),
      jnp.asarray(seg, BF16), jnp.asarray(pair, F32))


def _nsa_prep_kernel(q_ref, kc_ref, vc_ref, ks_ref, kw_ref, cos_ref, sin_ref,
                     qo_ref, kco_ref, vco_ref, kso_ref, kwo_ref):
    cos = cos_ref[...]
    sin = sin_ref[...]
    for h in range(N_HEADS):
        hs = slice(h * HEAD_DIM, (h + 1) * HEAD_DIM)
        qo_ref[:, hs] = _rope(q_ref[:, hs], cos, sin).astype(BF16)
    kco_ref[...] = _rope(kc_ref[...], cos, sin).astype(BF16)
    kso_ref[...] = _rope(ks_ref[...], cos, sin).astype(BF16)
    kwo_ref[...] = _rope(kw_ref[...], cos, sin).astype(BF16)
    vco_ref[...] = vc_ref[...]


def _nsa_prep(p32, p16, cos, sin, S):
    tm = min(S, 512)
    gw = GROUP_WIDTH // LANES
    row = lambda col: pl.BlockSpec((tm, HEAD_DIM), lambda i: (i, col))
    out = pl.BlockSpec((tm, HEAD_DIM), lambda i: (i, 0))
    return pl.pallas_call(
        _nsa_prep_kernel,
        grid=(S // tm,),
        in_specs=[pl.BlockSpec((tm, GROUP_WIDTH), lambda i: (i, COL_NQ // gw)),
                  row(COL_KC), row(COL_VC), row(COL_KS), row(COL_KW), out, out],
        out_specs=[pl.BlockSpec((tm, GROUP_WIDTH), lambda i: (i, 0)), out, out, out, out],
        out_shape=[jax.ShapeDtypeStruct((S, GROUP_WIDTH), BF16)]
        + [jax.ShapeDtypeStruct((S, HEAD_DIM), BF16)] * 4,
        compiler_params=_cparams(("parallel",), 32),
    )(p32, p32, p16, p32, p32, cos, sin)


def _nsa_compress_kernel(kb_ref, vb_ref, wk_ref, wv_ref, pk_ref, pv_ref, ko_ref, vo_ref):
    nblk = kb_ref.shape[0]
    half = NSA_CMP_STRIDE * HEAD_DIM
    rowid = lax.broadcasted_iota(jnp.int32, (nblk, HEAD_DIM), 0)
    for b_ref, w_ref, p_ref, o_ref in ((kb_ref, wk_ref, pk_ref, ko_ref), (vb_ref, wv_ref, pv_ref, vo_ref)):
        w = w_ref[...]
        top = _dot(b_ref[...], w[:half])
        bot = _dot(b_ref[...], w[half:])
        ph, pm, plo = _split3(p_ref[...])
        const = _dot(ph, w) + _dot(pm, w) + _dot(plo, w)
        shifted = pltpu.roll(bot, nblk - 1, axis=0)
        out = top + shifted + const
        o_ref[...] = jnp.where(rowid < nblk - 1, out, 0.0).astype(BF16)


def _nsa_compress(kc16, vc16, w_ck, w_cv, pos_k, pos_v, S):
    nblk = S // NSA_CMP_STRIDE
    width = NSA_CMP_STRIDE * HEAD_DIM
    kb = kc16.reshape(nblk, width)
    vb = vc16.reshape(nblk, width)
    wk = w_ck.reshape(NSA_CMP_LEN * HEAD_DIM, HEAD_DIM).astype(BF16)
    wv = w_cv.reshape(NSA_CMP_LEN * HEAD_DIM, HEAD_DIM).astype(BF16)
    full = lambda shape: pl.BlockSpec(shape, lambda i: (0,) * len(shape))
    return pl.pallas_call(
        _nsa_compress_kernel,
        grid=(1,),
        in_specs=[full((nblk, width)), full((nblk, width)),
                  full((2 * width, HEAD_DIM)), full((2 * width, HEAD_DIM)),
                  full((1, 2 * width)), full((1, 2 * width))],
        out_specs=[full((nblk, HEAD_DIM)), full((nblk, HEAD_DIM))],
        out_shape=[jax.ShapeDtypeStruct((nblk, HEAD_DIM), BF16)] * 2,
        compiler_params=_cparams(("arbitrary",), 32),
    )(kb, vb, wk, wv, pos_k.reshape(1, 2 * width), pos_v.reshape(1, 2 * width))


NSA_Q = 128
NSA_KT = 512


def _masked_softmax_rows(s, mask):
    s = jnp.where(mask, s, MASK_VALUE)
    m = jnp.max(s, axis=-1, keepdims=True)
    p = jnp.where(mask, jnp.exp(s - m), 0.0)
    return p / jnp.maximum(jnp.sum(p, axis=-1, keepdims=True), 1e-30)


def _nsa_attn_kernel(n_slc, q_ref, g_ref, kcmp_ref, vcmp_ref, ov_ref, ks_ref, vs_ref, kw_ref, vw_ref,
                     o_ref, m_ref, l_ref, acc_ref):
    Q = NSA_Q
    KT = NSA_KT
    bi = pl.program_id(0)
    q0 = bi * Q
    ncmp = kcmp_ref.shape[0]
    nsl = ov_ref.shape[1]
    qpos_col = q0 + lax.broadcasted_iota(jnp.int32, (Q, 1), 0)

    qs = [q_ref[:, h * HEAD_DIM:(h + 1) * HEAD_DIM] for h in range(N_HEADS)]

    cmp_idx = lax.broadcasted_iota(jnp.int32, (Q, ncmp), 1)
    cmp_mask = (cmp_idx * NSA_CMP_STRIDE + (NSA_CMP_LEN - 1) <= qpos_col) & (cmp_idx < ncmp - 1)
    kcmp = kcmp_ref[...]
    vcmp = vcmp_ref[...]
    o_cmp = []
    p_sum = jnp.zeros((Q, ncmp), F32)
    for h in range(N_HEADS):
        p = _masked_softmax_rows(_dot_nt(qs[h], kcmp) * ATTN_SCALE, cmp_mask)
        o_cmp.append(_dot(p.astype(BF16), vcmp))
        p_sum = p_sum + p
    ph = p_sum.astype(BF16)
    plo = (p_sum - ph.astype(F32)).astype(BF16)
    imp = _dot(ph, ov_ref[...]) + _dot(plo, ov_ref[...])

    blk = lax.broadcasted_iota(jnp.int32, (Q, nsl), 1)
    blkf = blk.astype(F32)
    cur = jnp.right_shift(qpos_col, NSA_SLC_SHIFT)
    forced = (blk == 0) | (blk == cur) | (blk == cur - 1)
    future = blk * NSA_SLC_LEN > qpos_col
    work = jnp.where(future, -1.0, jnp.where(forced, FORCED_SCORE, imp))
    work = jnp.where(blk < n_slc, work, -jnp.inf)
    sel = jnp.zeros((Q, nsl), F32)
    for _ in range(min(NSA_TOPK, n_slc)):
        mx = jnp.max(work, axis=-1, keepdims=True)
        first = jnp.min(jnp.where(work == mx, blkf, float(nsl)), axis=-1, keepdims=True)
        hit = blkf == first
        sel = jnp.where(hit, 1.0, sel)
        work = jnp.where(hit, -jnp.inf, work)
    sel = jnp.where(future, 0.0, sel).astype(BF16)

    m_ref[...] = jnp.full_like(m_ref, MASK_VALUE)
    l_ref[...] = jnp.zeros_like(l_ref)
    acc_ref[...] = jnp.zeros_like(acc_ref)
    n_tiles = (q0 + Q + KT - 1) // KT

    def slc_body(t, carry):
        start = pl.multiple_of(t * KT, KT)
        k = ks_ref[pl.ds(start, KT), :]
        v = vs_ref[pl.ds(start, KT), :]
        eb = lax.broadcasted_iota(jnp.int32, (nsl, KT), 0)
        ec = lax.broadcasted_iota(jnp.int32, (nsl, KT), 1)
        expand = jnp.where(eb == t * (KT // NSA_SLC_LEN) + jnp.right_shift(ec, NSA_SLC_SHIFT),
                           1.0, 0.0).astype(BF16)
        sel_tok = _dot(sel, expand)
        kpos = start + lax.broadcasted_iota(jnp.int32, (Q, KT), 1)
        mask = (sel_tok > 0.5) & (kpos <= qpos_col)
        for h in range(N_HEADS):
            s = jnp.where(mask, _dot_nt(qs[h], k) * ATTN_SCALE, MASK_VALUE)
            m_old = m_ref[h]
            m_new = jnp.maximum(m_old, jnp.max(s, axis=-1, keepdims=True))
            p = jnp.where(mask, jnp.exp(s - m_new), 0.0)
            alpha = jnp.exp(m_old - m_new)
            l_ref[h] = alpha * l_ref[h] + jnp.sum(p, axis=-1, keepdims=True)
            acc_ref[h] = alpha * acc_ref[h] + _dot(p.astype(BF16), v)
            m_ref[h] = m_new
        return carry

    lax.fori_loop(0, n_tiles, slc_body, 0)

    W = NSA_WINDOW + Q
    ws = pl.multiple_of(jnp.maximum(q0 - NSA_WINDOW, 0), Q)
    kwin = kw_ref[pl.ds(ws, W), :]
    vwin = vw_ref[pl.ds(ws, W), :]
    wpos = ws + lax.broadcasted_iota(jnp.int32, (Q, W), 1)
    wmask = (wpos <= qpos_col) & (qpos_col - wpos < NSA_WINDOW)

    gate = jax.nn.sigmoid(g_ref[...])
    for h in range(N_HEADS):
        pw = _masked_softmax_rows(_dot_nt(qs[h], kwin) * ATTN_SCALE, wmask)
        o_win = _dot(pw.astype(BF16), vwin)
        o_slc = acc_ref[h] / jnp.maximum(l_ref[h], 1e-30)
        out = (gate[:, 3 * h:3 * h + 1] * o_cmp[h] + gate[:, 3 * h + 1:3 * h + 2] * o_slc
               + gate[:, 3 * h + 2:3 * h + 3] * o_win)
        o_ref[:, h * HEAD_DIM:(h + 1) * HEAD_DIM] = out.astype(o_ref.dtype)


def _nsa_overlap(S):
    n_cmp = (S - NSA_CMP_LEN) // NSA_CMP_STRIDE + 1
    n_slc = S // NSA_SLC_LEN
    nblk = S // NSA_CMP_STRIDE
    nsl = max(LANES, n_slc)
    cmp_start = np.arange(n_cmp) * NSA_CMP_STRIDE
    slc_start = np.arange(n_slc) * NSA_SLC_LEN
    ov = (np.minimum(cmp_start[:, None] + NSA_CMP_LEN, slc_start[None, :] + NSA_SLC_LEN)
          - np.maximum(cmp_start[:, None], slc_start[None, :]))
    out = np.zeros((nblk, nsl), np.float32)
    out[:n_cmp, :n_slc] = np.clip(ov, 0, None) / NSA_CMP_LEN
    return out, n_slc


def _nsa_attention(qn16, p32, p16, kcmp, vcmp, ks16, kw16, S):
    Q = NSA_Q
    ov, n_slc = _nsa_overlap(S)
    nblk, nsl = ov.shape
    full = lambda shape: pl.BlockSpec(shape, lambda i: (0,) * len(shape))
    return pl.pallas_call(
        functools.partial(_nsa_attn_kernel, n_slc),
        grid=(S // Q,),
        in_specs=[pl.BlockSpec((Q, GROUP_WIDTH), lambda i: (i, 0)),
                  pl.BlockSpec((Q, HEAD_DIM), lambda i: (i, COL_NG)),
                  full((nblk, HEAD_DIM)), full((nblk, HEAD_DIM)), full((nblk, nsl)),
                  full((S, HEAD_DIM)),
                  pl.BlockSpec((S, HEAD_DIM), lambda i: (0, COL_VS)),
                  full((S, HEAD_DIM)),
                  pl.BlockSpec((S, HEAD_DIM), lambda i: (0, COL_VW))],
        out_specs=pl.BlockSpec((Q, GROUP_WIDTH), lambda i: (i, 0)),
        out_shape=jax.ShapeDtypeStruct((S, GROUP_WIDTH), BF16),
        scratch_shapes=[pltpu.VMEM((N_HEADS, Q, 1), F32), pltpu.VMEM((N_HEADS, Q, 1), F32),
                        pltpu.VMEM((N_HEADS, Q, HEAD_DIM), F32)],
        compiler_params=_cparams(("arbitrary",), 40),
    )(qn16, p32, kcmp, vcmp, jnp.asarray(ov, BF16), ks16, p16, kw16, p16)


def _outproj_kernel(a0_ref, a1_ref, a2_ref, a3_ref, w_ref, x_ref, o_ref):
    acc = x_ref[...]
    for g, a_ref in enumerate((a0_ref, a1_ref, a2_ref, a3_ref)):
        acc = acc + _dot(a_ref[...], w_ref[g * GROUP_WIDTH:(g + 1) * GROUP_WIDTH, :])
    o_ref[...] = acc


def _outproj(parts, w16, x, tm):
    S, N = x.shape
    a_spec = pl.BlockSpec((tm, GROUP_WIDTH), lambda i: (i, 0))
    return pl.pallas_call(
        _outproj_kernel,
        grid=(S // tm,),
        in_specs=[a_spec, a_spec, a_spec, a_spec,
                  pl.BlockSpec((4 * GROUP_WIDTH, N), lambda i: (0, 0)),
                  pl.BlockSpec((tm, N), lambda i: (i, 0))],
        out_specs=pl.BlockSpec((tm, N), lambda i: (i, 0)),
        out_shape=jax.ShapeDtypeStruct((S, N), F32),
        compiler_params=_cparams(("parallel",), 48),
    )(*parts, w16, x)


def _xattn_kernel(x_ref, g_ref, wq_ref, k_ref, v_ref, wo_ref, o_ref):
    x = x_ref[...]
    h = _rms(x, g_ref[...]).astype(BF16)
    q = _dot(h, wq_ref[...]).astype(BF16)
    outs = []
    for hd in range(N_HEADS):
        hs = slice(hd * HEAD_DIM, (hd + 1) * HEAD_DIM)
        s = _dot_nt(q[:, hs], k_ref[:, hs]) * ATTN_SCALE
        m = jnp.max(s, axis=-1, keepdims=True)
        p = jnp.exp(s - m)
        p = p / jnp.sum(p, axis=-1, keepdims=True)
        outs.append(_dot(p.astype(BF16), v_ref[:, hs]).astype(BF16))
    o = jnp.concatenate(outs, axis=-1)
    o_ref[...] = x + _dot(o, wo_ref[...])


def _xattn(x, gain, wq16, k16, v16, wo16, tm):
    S, K = x.shape
    XW = wq16.shape[1]
    M = k16.shape[0]
    return pl.pallas_call(
        _xattn_kernel,
        grid=(S // tm,),
        in_specs=[pl.BlockSpec((tm, K), lambda i: (i, 0)),
                  pl.BlockSpec((1, K), lambda i: (0, 0)),
                  pl.BlockSpec((K, XW), lambda i: (0, 0)),
                  pl.BlockSpec((M, XW), lambda i: (0, 0)),
                  pl.BlockSpec((M, XW), lambda i: (0, 0)),
                  pl.BlockSpec((XW, K), lambda i: (0, 0))],
        out_specs=pl.BlockSpec((tm, K), lambda i: (i, 0)),
        out_shape=jax.ShapeDtypeStruct((S, K), F32),
        compiler_params=_cparams(("parallel",), 48),
    )(x, gain.reshape(1, K), wq16, k16, v16, wo16)


FFN_HALO = 16


def _ffn_kernel(x_ref, xp_ref, g_ref, wg_ref, wv_ref, cwg_ref, cwv_ref, cbg_ref, cbv_ref, wd_ref,
                o_ref, h_ref, acc_ref):
    i = pl.program_id(0)
    f = pl.program_id(1)
    tm = x_ref.shape[0]

    @pl.when(f == 0)
    def _():
        hp = _rms(xp_ref[...], g_ref[...])
        h_ref[0:FFN_HALO, :] = jnp.where(i > 0, hp, 0.0).astype(BF16)
        h_ref[FFN_HALO:, :] = _rms(x_ref[...], g_ref[...]).astype(BF16)
        acc_ref[...] = x_ref[...]

    h = h_ref[...]

    def conv(w_ref, cw_ref, cb_ref):
        u = _dot(h, w_ref[...])
        u1 = pltpu.roll(u, 1, axis=0)
        u2 = pltpu.roll(u, 2, axis=0)
        c = cb_ref[...] + cw_ref[2:3, :] * u + cw_ref[1:2, :] * u1 + cw_ref[0:1, :] * u2
        return c[FFN_HALO:, :]

    gate = conv(wg_ref, cwg_ref, cbg_ref)
    val = conv(wv_ref, cwv_ref, cbv_ref)
    act = (_silu(gate) * val).astype(BF16)
    acc_ref[...] += _dot(act, wd_ref[...])

    @pl.when(f == pl.num_programs(1) - 1)
    def _():
        o_ref[...] = acc_ref[...]


def _ffn(x, gain, w_up16, conv_w, conv_b, w_down16, tm, tf):
    S, K = x.shape
    F = w_down16.shape[0]
    nf = F // tf
    hb = tm // FFN_HALO
    cb = conv_b.reshape(1, 2 * F)
    return pl.pallas_call(
        _ffn_kernel,
        grid=(S // tm, nf),
        in_specs=[pl.BlockSpec((tm, K), lambda i, f: (i, 0)),
                  pl.BlockSpec((FFN_HALO, K), lambda i, f: (jnp.maximum(i * hb - 1, 0), 0)),
                  pl.BlockSpec((1, K), lambda i, f: (0, 0)),
                  pl.BlockSpec((K, tf), lambda i, f: (0, f)),
                  pl.BlockSpec((K, tf), lambda i, f: (0, nf + f)),
                  pl.BlockSpec((3, tf), lambda i, f: (0, f)),
                  pl.BlockSpec((3, tf), lambda i, f: (0, nf + f)),
                  pl.BlockSpec((1, tf), lambda i, f: (0, f)),
                  pl.BlockSpec((1, tf), lambda i, f: (0, nf + f)),
                  pl.BlockSpec((tf, K), lambda i, f: (f, 0))],
        out_specs=pl.BlockSpec((tm, K), lambda i, f: (i, 0)),
        out_shape=jax.ShapeDtypeStruct((S, K), F32),
        scratch_shapes=[pltpu.VMEM((tm + FFN_HALO, K), BF16), pltpu.VMEM((tm, K), F32)],
        compiler_params=_cparams(("parallel", "arbitrary"), 52),
    )(x, x, gain.reshape(1, K), w_up16, w_up16, conv_w, conv_w, cb, cb, w_down16)


def _final_norm_kernel(x_ref, g_ref, o_ref):
    o_ref[...] = _rms(x_ref[...], g_ref[...])


def _final_norm(x, gain, tm):
    S, K = x.shape
    return pl.pallas_call(
        _final_norm_kernel,
        grid=(S // tm,),
        in_specs=[pl.BlockSpec((tm, K), lambda i: (i, 0)), pl.BlockSpec((1, K), lambda i: (0, 0))],
        out_specs=pl.BlockSpec((tm, K), lambda i: (i, 0)),
        out_shape=jax.ShapeDtypeStruct((S, K), F32),
        compiler_params=_cparams(("parallel",), 40),
    )(x, gain.reshape(1, K))


def _mixers(p32, p16, cos, sin, layer, S, ret_norm, hgrn_lb_logits, hgrn_norm,
            nsa_pos_k, nsa_pos_v, nsa_w_ck, nsa_w_cv):
    o_ret = _retention(p32, p16, cos, sin, ret_norm, S)
    o_sb = _stick_breaking(p16, S)
    o_hg = _hgrn2(p32, p16, hgrn_lb_logits, hgrn_norm, layer, S)
    qn16, kc16, vc16, ks16, kw16 = _nsa_prep(p32, p16, cos, sin, S)
    kcmp, vcmp = _nsa_compress(kc16, vc16, nsa_w_ck, nsa_w_cv, nsa_pos_k, nsa_pos_v, S)
    o_nsa = _nsa_attention(qn16, p32, p16, kcmp, vcmp, ks16, kw16, S)
    return o_ret, o_sb, o_hg, o_nsa


def kernel(x, mem, positions, mix_norm, w_in, ret_norm, hgrn_lb_logits, hgrn_norm, nsa_pos_k, nsa_pos_v,
           nsa_w_ck, nsa_w_cv, w_out, xattn_norm, mem_norm, xattn_wq, xattn_wk, xattn_wv, xattn_wo,
           ffn_norm, ffn_w_up, ffn_conv_w, ffn_conv_b, ffn_w_down, final_norm):
    B, S, _ = x.shape
    assert B == 1 and S % 512 == 0
    xs = x.reshape(S, D_MODEL)
    mem2 = mem.reshape(N_MEM, D_MODEL)
    cos, sin = _rope_tables(positions, S)
    tm = min(S, 512)
    for layer in range(DEPTH):
        w_in16 = jnp.pad(w_in[layer].astype(BF16), ((0, 0), (0, IN_COLS_PAD - IN_COLS)))
        p32, p16 = _inproj(xs, mix_norm[layer], w_in16, min(S, 1024), 1024)
        parts = _mixers(p32, p16, cos, sin, layer, S, ret_norm[layer], hgrn_lb_logits,
                        hgrn_norm[layer], nsa_pos_k[layer], nsa_pos_v[layer],
                        nsa_w_ck[layer], nsa_w_cv[layer])
        xs = _outproj(parts, w_out[layer].astype(BF16), xs, tm)
        k16 = _normproj(mem2, mem_norm[layer], xattn_wk[layer].astype(BF16), BF16)
        v16 = _normproj(mem2, mem_norm[layer], xattn_wv[layer].astype(BF16), BF16)
        xs = _xattn(xs, xattn_norm[layer], xattn_wq[layer].astype(BF16), k16, v16,
                    xattn_wo[layer].astype(BF16), tm)
        xs = _ffn(xs, ffn_norm[layer], ffn_w_up[layer].astype(BF16), ffn_conv_w[layer],
                  ffn_conv_b[layer], ffn_w_down[layer].astype(BF16), tm, 512)
    return _final_norm(xs, final_norm, tm).reshape(B, S, D_MODEL)
```

```python
import functools
import math

import numpy as np
import jax
import jax.numpy as jnp
from jax import lax
from jax.experimental import pallas as pl
from jax.experimental.pallas import tpu as pltpu

F32 = jnp.float32
BF16 = jnp.bfloat16

D_MODEL = 2048
DEPTH = 4
N_HEADS = 4
HEAD_DIM = 128
GROUP_WIDTH = N_HEADS * HEAD_DIM
ROPE_THETA = 10000.0
RET_CHUNK = 128
HGRN_CHUNK = 64
NSA_CMP_LEN = 32
NSA_CMP_STRIDE = 16
NSA_SLC_LEN = 64
NSA_SLC_SHIFT = 6
NSA_TOPK = 16
NSA_WINDOW = 512
N_MEM = 256
D_FF = 5632
NORM_EPS = 1e-6
MASK_VALUE = -1e30
FORCED_SCORE = 1e4
MIN_FORGET = 1e-6
IN_COLS = 12 * GROUP_WIDTH + 6 * HEAD_DIM + 3 * N_HEADS
IN_COLS_PAD = 7168
ATTN_SCALE = HEAD_DIM ** -0.5

COL_RQ, COL_RK, COL_RV, COL_RG = 0, 4, 8, 12
COL_SQ, COL_SK, COL_SV = 16, 20, 24
COL_GQ, COL_GF, COL_GI, COL_GG = 28, 32, 36, 40
COL_NQ = 44
COL_KC, COL_VC, COL_KS, COL_VS, COL_KW, COL_VW, COL_NG = 48, 49, 50, 51, 52, 53, 54

V7X_VMEM_BYTES = 64 * 1024 * 1024
LANES = 128

PROJ_TM, PROJ_TN = 1024, 1024
ROW_TM = 512
FFN_TM, FFN_TF = 1024, 512


def _cparams(semantics, vmem_mb):
    return pltpu.CompilerParams(dimension_semantics=semantics,
                                vmem_limit_bytes=min(vmem_mb * 1024 * 1024, V7X_VMEM_BYTES - (8 << 20)))


def _layer_spec(layer, tail_shape, tail_index=None):
    if tail_index is None:
        tail_index = lambda *ids: (0,) * len(tail_shape)
    return pl.BlockSpec((None,) + tuple(tail_shape), lambda *ids: (layer,) + tuple(tail_index(*ids)))


def _dot(a, b):
    return jnp.dot(a, b, preferred_element_type=F32)


def _dot_nt(a, b):
    return lax.dot_general(a, b, (((1,), (1,)), ((), ())), preferred_element_type=F32)


def _dot_tn(a, b):
    return lax.dot_general(a, b, (((0,), (0,)), ((), ())), preferred_element_type=F32)


def _split3(x):
    h1 = x.astype(BF16)
    r1 = x - h1.astype(F32)
    h2 = r1.astype(BF16)
    r2 = r1 - h2.astype(F32)
    return h1, h2, r2.astype(BF16)


def _dot_exact_lhs(a_bf16, x_f32):
    h1, h2, h3 = _split3(x_f32)
    return _dot(a_bf16, h1) + _dot(a_bf16, h2) + _dot(a_bf16, h3)


def _rms(x, gain):
    ms = jnp.mean(x * x, axis=-1, keepdims=True)
    return x * lax.rsqrt(ms + NORM_EPS) * gain


def _rope(x, cos, sin_signed):
    return x * cos + pltpu.roll(x, HEAD_DIM // 2, axis=1) * sin_signed


def _silu(x):
    return x * jax.nn.sigmoid(x)


def _rope_table_kernel(pos_ref, inv_ref, cos_ref, sin_ref):
    ang = pos_ref[...].astype(F32) * inv_ref[...]
    lane = lax.broadcasted_iota(jnp.int32, ang.shape, 1)
    cos_ref[...] = jnp.cos(ang)
    s = jnp.sin(ang)
    sin_ref[...] = jnp.where(lane < HEAD_DIM // 2, -s, s)


def _rope_tables(positions, S):
    half = HEAD_DIM // 2
    inv = ROPE_THETA ** (-jnp.arange(half, dtype=F32) / half)
    inv = jnp.concatenate([inv, inv]).reshape(1, HEAD_DIM)
    pos = positions.reshape(S, 1)
    tm = min(S, 1024)
    return pl.pallas_call(
        _rope_table_kernel,
        grid=(S // tm,),
        in_specs=[pl.BlockSpec((tm, 1), lambda i: (i, 0)),
                  pl.BlockSpec((1, HEAD_DIM), lambda i: (0, 0))],
        out_specs=[pl.BlockSpec((tm, HEAD_DIM), lambda i: (i, 0)),
                   pl.BlockSpec((tm, HEAD_DIM), lambda i: (i, 0))],
        out_shape=[jax.ShapeDtypeStruct((S, HEAD_DIM), F32)] * 2,
        compiler_params=_cparams(("arbitrary",), 32),
    )(pos, inv)


def _inproj_kernel(x_ref, g_ref, w_ref, o32_ref, o16_ref, h_ref):
    @pl.when(pl.program_id(1) == 0)
    def _():
        h_ref[...] = _rms(x_ref[...], g_ref[...]).astype(BF16)

    y = _dot(h_ref[...], w_ref[...])
    o32_ref[...] = y
    o16_ref[...] = y.astype(BF16)


def _inproj(x, gains, w16, layer, tm, tn):
    S, K = x.shape
    N = w16.shape[2]
    return pl.pallas_call(
        _inproj_kernel,
        grid=(S // tm, N // tn),
        in_specs=[pl.BlockSpec((tm, K), lambda i, j: (i, 0)),
                  _layer_spec(layer, (1, K)),
                  _layer_spec(layer, (K, tn), lambda i, j: (0, j))],
        out_specs=[pl.BlockSpec((tm, tn), lambda i, j: (i, j)),
                   pl.BlockSpec((tm, tn), lambda i, j: (i, j))],
        out_shape=[jax.ShapeDtypeStruct((S, N), F32), jax.ShapeDtypeStruct((S, N), BF16)],
        scratch_shapes=[pltpu.VMEM((tm, K), BF16)],
        compiler_params=_cparams(("parallel", "arbitrary"), 48),
    )(x, gains, w16)


def _normproj_kernel(x_ref, g_ref, w_ref, o_ref):
    h = _rms(x_ref[...], g_ref[...]).astype(BF16)
    o_ref[...] = _dot(h, w_ref[...]).astype(o_ref.dtype)


def _normproj(x, gains, w16, layer, out_dtype):
    M, K = x.shape
    N = w16.shape[2]
    return pl.pallas_call(
        _normproj_kernel,
        grid=(1,),
        in_specs=[pl.BlockSpec((M, K), lambda i: (0, 0)),
                  _layer_spec(layer, (1, K)),
                  _layer_spec(layer, (K, N))],
        out_specs=pl.BlockSpec((M, N), lambda i: (0, 0)),
        out_shape=jax.ShapeDtypeStruct((M, N), out_dtype),
        compiler_params=_cparams(("arbitrary",), 32),
    )(x, gains, w16)


_RET_LOG_GAMMA = [math.log1p(-(2.0 ** (-5.0 - h))) for h in range(N_HEADS)]


def _ret_kernel(q_ref, k_ref, v_ref, g_ref, cos_ref, sin_ref, gain_ref, o_ref, st_ref):
    C = RET_CHUNK

    @pl.when(pl.program_id(0) == 0)
    def _():
        st_ref[...] = jnp.zeros_like(st_ref)

    cos = cos_ref[...]
    sin = sin_ref[...]
    row = lax.broadcasted_iota(jnp.int32, (C, C), 0)
    col = lax.broadcasted_iota(jnp.int32, (C, C), 1)
    rel = (row - col).astype(F32)
    rowf = row.astype(F32)
    for h in range(N_HEADS):
        hs = slice(h * HEAD_DIM, (h + 1) * HEAD_DIM)
        lg = _RET_LOG_GAMMA[h]
        decay = jnp.where(rel >= 0, jnp.exp(lg * jnp.maximum(rel, 0.0)), 0.0)
        q = _rope(q_ref[:, hs], cos, sin)
        k = _rope(k_ref[:, hs], cos, sin) * ATTN_SCALE
        v = v_ref[:, hs]
        st = st_ref[h]
        s = _dot_nt(q.astype(BF16), k.astype(BF16)) * decay
        o = _dot(s.astype(BF16), v)
        q_w = jnp.exp(lg * (rowf + 1.0))
        o = o + _dot_nt((q * q_w).astype(BF16), st.astype(BF16))
        k_w = jnp.exp(lg * (C - 1.0 - rowf))
        st_ref[h] = st * math.exp(lg * C) + _dot_tn(v, (k * k_w).astype(BF16))
        oc = o - jnp.mean(o, axis=-1, keepdims=True)
        y = oc * lax.rsqrt(jnp.mean(oc * oc, axis=-1, keepdims=True) + NORM_EPS) * gain_ref[:, hs]
        o_ref[:, hs] = (y * _silu(g_ref[:, hs])).astype(o_ref.dtype)


def _retention(p32, p16, cos, sin, gains, layer, S):
    C = RET_CHUNK
    gw = GROUP_WIDTH // LANES
    return pl.pallas_call(
        _ret_kernel,
        grid=(S // C,),
        in_specs=[pl.BlockSpec((C, GROUP_WIDTH), lambda c: (c, COL_RQ // gw)),
                  pl.BlockSpec((C, GROUP_WIDTH), lambda c: (c, COL_RK // gw)),
                  pl.BlockSpec((C, GROUP_WIDTH), lambda c: (c, COL_RV // gw)),
                  pl.BlockSpec((C, GROUP_WIDTH), lambda c: (c, COL_RG // gw)),
                  pl.BlockSpec((C, HEAD_DIM), lambda c: (c, 0)),
                  pl.BlockSpec((C, HEAD_DIM), lambda c: (c, 0)),
                  _layer_spec(layer, (1, GROUP_WIDTH))],
        out_specs=pl.BlockSpec((C, GROUP_WIDTH), lambda c: (c, 0)),
        out_shape=jax.ShapeDtypeStruct((S, GROUP_WIDTH), BF16),
        scratch_shapes=[pltpu.VMEM((N_HEADS, HEAD_DIM, HEAD_DIM), F32)],
        compiler_params=_cparams(("arbitrary",), 32),
    )(p32, p32, p16, p32, cos, sin, gains)


SB_TILE = 256
SB_DEAD_LOG = -104.0


def _sb_tile(q, k_ref, v_ref, j, run, acc, upper, diag):
    T = SB_TILE
    start = pl.multiple_of(j * T, T)
    k = k_ref[pl.ds(start, T), :]
    v = v_ref[pl.ds(start, T), :]
    z = _dot_nt(q, k) * ATTN_SCALE
    sp = jnp.maximum(z, 0.0) + jnp.log(1.0 + jnp.exp(-jnp.abs(z)))
    log_1m = -sp
    if diag:
        row = lax.broadcasted_iota(jnp.int32, (T, T), 0)
        col = lax.broadcasted_iota(jnp.int32, (T, T), 1)
        strict = col < row
        log_1m = jnp.where(strict, log_1m, 0.0)
    hi = log_1m.astype(BF16)
    lo = (log_1m - hi.astype(F32)).astype(BF16)
    both = _dot(jnp.concatenate([hi, lo], axis=0), upper)
    between = both[:T] + both[T:] + run
    w = jnp.exp((z - sp) + between)
    if diag:
        w = jnp.where(strict, w, 0.0)
    acc = acc + _dot(w.astype(BF16), v)
    run = between[:, 0:1] + log_1m[:, 0:1]
    return run, acc


def _sb_kernel(q_ref, k_ref, v_ref, o_ref):
    T = SB_TILE
    i = pl.program_id(1)
    q = q_ref[...]
    r = lax.broadcasted_iota(jnp.int32, (T, T), 0)
    c = lax.broadcasted_iota(jnp.int32, (T, T), 1)
    upper = jnp.where(r > c, 1.0, 0.0).astype(BF16)
    run = jnp.zeros((T, 1), F32)
    acc = jnp.zeros((T, HEAD_DIM), F32)
    run, acc = _sb_tile(q, k_ref, v_ref, i, run, acc, upper, True)

    def alive(run):
        return (jnp.max(run) > SB_DEAD_LOG).astype(jnp.int32)

    def cond(carry):
        step, live, _, _ = carry
        return jnp.logical_and(step < i, live > 0)

    def body(carry):
        step, _, run, acc = carry
        run, acc = _sb_tile(q, k_ref, v_ref, i - 1 - step, run, acc, upper, False)
        return step + 1, alive(run), run, acc

    _, _, _, acc = lax.while_loop(cond, body, (jnp.int32(0), alive(run), run, acc))
    o_ref[...] = acc.astype(o_ref.dtype)


def _stick_breaking(p16, S):
    T = SB_TILE
    return pl.pallas_call(
        _sb_kernel,
        grid=(N_HEADS, S // T),
        in_specs=[pl.BlockSpec((T, HEAD_DIM), lambda h, i: (i, COL_SQ + h)),
                  pl.BlockSpec((S, HEAD_DIM), lambda h, i: (0, COL_SK + h)),
                  pl.BlockSpec((S, HEAD_DIM), lambda h, i: (0, COL_SV + h))],
        out_specs=pl.BlockSpec((T, HEAD_DIM), lambda h, i: (i, h)),
        out_shape=jax.ShapeDtypeStruct((S, GROUP_WIDTH), BF16),
        compiler_params=_cparams(("parallel", "arbitrary"), 40),
    )(p16, p16, p16)


_HG_LEVELS = (32, 16, 8, 4, 2, 1)


def _hgrn_constants():
    C = HGRN_CHUNK
    n = np.arange(C)
    seg = np.zeros((8 * C, C), np.float32)
    pair = np.zeros((7, C, C), np.float32)
    for l, s in enumerate(_HG_LEVELS):
        blk = n // s
        odd = (blk % 2) == 1
        a = blk * s
        e = a + s - 1
        j = n[None, :]
        t_odd = (j >= a[:, None]) & (j <= n[:, None])
        t_even = (j > n[:, None]) & (j <= e[:, None])
        seg[l * C:(l + 1) * C] = np.where(odd[:, None], t_odd, t_even)
        pair[l] = odd[:, None] & (blk[None, :] == blk[:, None] - 1)
    seg[6 * C:7 * C] = n[None, :] <= n[:, None]
    seg[7 * C:8 * C] = n[None, :] > n[:, None]
    pair[6] = np.eye(C)
    return seg, pair


def _hgrn_kernel(layer, q_ref, f_ref, v_ref, g_ref, lbl_ref, gain_ref, seg_ref, pair_ref, o_ref, st_ref):
    C = HGRN_CHUNK

    @pl.when(pl.program_id(0) == 0)
    def _():
        st_ref[...] = jnp.zeros_like(st_ref)

    logits = lbl_ref[...]
    e = jnp.exp(logits - jnp.max(logits, axis=0, keepdims=True))
    p = e / jnp.sum(e, axis=0, keepdims=True)
    lb = jnp.zeros((1, GROUP_WIDTH), F32)
    for l in range(1, layer + 1):
        lb = lb + p[l:l + 1, :]

    f = lb + (1.0 - lb) * jax.nn.sigmoid(f_ref[...])
    log_f = jnp.log(jnp.maximum(f, MIN_FORGET))
    kk = 1.0 - f
    ex = jnp.exp(_dot_exact_lhs(seg_ref[...], log_f))

    for h in range(N_HEADS):
        hs = slice(h * HEAD_DIM, (h + 1) * HEAD_DIM)
        q = q_ref[:, hs] * ATTN_SCALE
        k = kk[:, hs]
        v = v_ref[:, hs]
        s = pair_ref[6] * _dot_nt(q.astype(BF16), k.astype(BF16))
        for l in range(6):
            el = ex[l * C:(l + 1) * C, hs]
            s = s + pair_ref[l] * _dot_nt((q * el).astype(BF16), (k * el).astype(BF16))
        e_cum = ex[6 * C:7 * C, hs]
        e_rest = ex[7 * C:8 * C, hs]
        st = st_ref[h]
        o = _dot(s.astype(BF16), v) + _dot_nt((q * e_cum).astype(BF16), st.astype(BF16))
        st_ref[h] = st * e_cum[C - 1:C, :] + _dot_tn(v, (k * e_rest).astype(BF16))
        y = o * lax.rsqrt(jnp.mean(o * o, axis=-1, keepdims=True) + NORM_EPS) * gain_ref[:, hs]
        o_ref[:, hs] = (y * _silu(g_ref[:, hs])).astype(o_ref.dtype)


def _hgrn2(p32, p16, lb_logits, gains, layer, S):
    C = HGRN_CHUNK
    gw = GROUP_WIDTH // LANES
    seg, pair = _hgrn_constants()
    return pl.pallas_call(
        functools.partial(_hgrn_kernel, layer),
        grid=(S // C,),
        in_specs=[pl.BlockSpec((C, GROUP_WIDTH), lambda c: (c, COL_GQ // gw)),
                  pl.BlockSpec((C, GROUP_WIDTH), lambda c: (c, COL_GF // gw)),
                  pl.BlockSpec((C, GROUP_WIDTH), lambda c: (c, COL_GI // gw)),
                  pl.BlockSpec((C, GROUP_WIDTH), lambda c: (c, COL_GG // gw)),
                  pl.BlockSpec((DEPTH, GROUP_WIDTH), lambda c: (0, 0)),
                  _layer_spec(layer, (1, GROUP_WIDTH)),
                  pl.BlockSpec((8 * C, C), lambda c: (0, 0)),
                  pl.BlockSpec((7, C, C), lambda c: (0, 0, 0))],
        out_specs=pl.BlockSpec((C, GROUP_WIDTH), lambda c: (c, 0)),
        out_shape=jax.ShapeDtypeStruct((S, GROUP_WIDTH), BF16),
        scratch_shapes=[pltpu.VMEM((N_HEADS, HEAD_DIM, HEAD_DIM), F32)],
        compiler_params=_cparams(("arbitrary",), 32),
    )(p32, p32, p16, p32, lb_logits, gains,
      jnp.asarray(seg, BF16), jnp.asarray(pair, F32))


NSA_Q_SCALE = ATTN_SCALE * math.log2(math.e)
NSA_MAX_SLC = LANES


def _nsa_prep_kernel(q_ref, kc_ref, vc_ref, ks_ref, kw_ref, vs_ref, vw_ref, cos_ref, sin_ref,
                     qo_ref, kco_ref, vco_ref, kso_ref, kwo_ref, vso_ref, vwo_ref):
    tm = q_ref.shape[0]
    ones = jnp.ones((tm, HEAD_DIM), BF16)
    vso_ref[:, 0:HEAD_DIM] = vs_ref[...]
    vso_ref[:, HEAD_DIM:] = ones
    vwo_ref[:, 0:HEAD_DIM] = vw_ref[...]
    vwo_ref[:, HEAD_DIM:] = ones
    cos = cos_ref[...]
    sin = sin_ref[...]
    for h in range(N_HEADS):
        hs = slice(h * HEAD_DIM, (h + 1) * HEAD_DIM)
        qo_ref[:, hs] = (_rope(q_ref[:, hs], cos, sin) * NSA_Q_SCALE).astype(BF16)
    kco_ref[...] = _rope(kc_ref[...], cos, sin).astype(BF16)
    kwo_ref[...] = _rope(kw_ref[...], cos, sin).astype(BF16)
    vco_ref[...] = vc_ref[...]
    kso_ref[:, 0:HEAD_DIM] = _rope(ks_ref[...], cos, sin).astype(BF16)
    tok = pl.program_id(0) * tm + lax.broadcasted_iota(jnp.int32, (tm, NSA_MAX_SLC), 0)
    lane = lax.broadcasted_iota(jnp.int32, (tm, NSA_MAX_SLC), 1)
    kso_ref[:, HEAD_DIM:] = jnp.where(jnp.right_shift(tok, NSA_SLC_SHIFT) == lane, 1.0, 0.0).astype(BF16)


def _nsa_prep(p32, p16, cos, sin, S):
    tm = min(S, 512)
    gw = GROUP_WIDTH // LANES
    row = lambda col: pl.BlockSpec((tm, HEAD_DIM), lambda i: (i, col))
    out = pl.BlockSpec((tm, HEAD_DIM), lambda i: (i, 0))
    aug = pl.BlockSpec((tm, HEAD_DIM + NSA_MAX_SLC), lambda i: (i, 0))
    return pl.pallas_call(
        _nsa_prep_kernel,
        grid=(S // tm,),
        in_specs=[pl.BlockSpec((tm, GROUP_WIDTH), lambda i: (i, COL_NQ // gw)),
                  row(COL_KC), row(COL_VC), row(COL_KS), row(COL_KW), row(COL_VS), row(COL_VW), out, out],
        out_specs=[pl.BlockSpec((tm, GROUP_WIDTH), lambda i: (i, 0)), out, out, aug, out, aug, aug],
        out_shape=[jax.ShapeDtypeStruct((S, GROUP_WIDTH), BF16),
                   jax.ShapeDtypeStruct((S, HEAD_DIM), BF16),
                   jax.ShapeDtypeStruct((S, HEAD_DIM), BF16),
                   jax.ShapeDtypeStruct((S, HEAD_DIM + NSA_MAX_SLC), BF16),
                   jax.ShapeDtypeStruct((S, HEAD_DIM), BF16),
                   jax.ShapeDtypeStruct((S, 2 * HEAD_DIM), BF16),
                   jax.ShapeDtypeStruct((S, 2 * HEAD_DIM), BF16)],
        compiler_params=_cparams(("parallel",), 32),
    )(p32, p32, p16, p32, p32, p16, p16, cos, sin)


def _nsa_compress_kernel(kb_ref, vb_ref, wk_ref, wv_ref, pk_ref, pv_ref, ko_ref, vo_ref):
    nblk = kb_ref.shape[0]
    half = NSA_CMP_STRIDE * HEAD_DIM
    rowid = lax.broadcasted_iota(jnp.int32, (nblk, HEAD_DIM), 0)
    for b_ref, w_ref, p_ref, o_ref in ((kb_ref, wk_ref, pk_ref, ko_ref), (vb_ref, wv_ref, pv_ref, vo_ref)):
        w = w_ref[...]
        top = _dot(b_ref[...], w[:half])
        bot = _dot(b_ref[...], w[half:])
        ph, pm, plo = _split3(p_ref[...])
        const = _dot(ph, w) + _dot(pm, w) + _dot(plo, w)
        shifted = pltpu.roll(bot, nblk - 1, axis=0)
        out = top + shifted + const
        o_ref[...] = jnp.where(rowid < nblk - 1, out, 0.0).astype(BF16)


def _nsa_compress(kc16, vc16, wk16, wv16, pos_k, pos_v, layer, S):
    nblk = S // NSA_CMP_STRIDE
    width = NSA_CMP_STRIDE * HEAD_DIM
    kb = kc16.reshape(nblk, width)
    vb = vc16.reshape(nblk, width)
    full = lambda shape: pl.BlockSpec(shape, lambda i: (0,) * len(shape))
    return pl.pallas_call(
        _nsa_compress_kernel,
        grid=(1,),
        in_specs=[full((nblk, width)), full((nblk, width)),
                  _layer_spec(layer, (2 * width, HEAD_DIM)), _layer_spec(layer, (2 * width, HEAD_DIM)),
                  _layer_spec(layer, (1, 2 * width)), _layer_spec(layer, (1, 2 * width))],
        out_specs=[full((nblk, HEAD_DIM)), full((nblk, HEAD_DIM))],
        out_shape=[jax.ShapeDtypeStruct((nblk, HEAD_DIM), BF16)] * 2,
        compiler_params=_cparams(("arbitrary",), 32),
    )(kb, vb, wk16, wv16, pos_k, pos_v)


NSA_Q = 128
NSA_KT = 512
NSA_ROWS = N_HEADS * NSA_Q
NSA_OFF = -(2.0 ** 126)


def _lanes4(x):
    return jnp.maximum(jnp.maximum(x[:, 0:LANES], x[:, LANES:2 * LANES]),
                       jnp.maximum(x[:, 2 * LANES:3 * LANES], x[:, 3 * LANES:4 * LANES]))


def _nsa_attn_kernel(n_slc, q_ref, g_ref, kcmp_ref, vcmp_ref, ovt_ref, ks_ref, vs_ref, kw_ref, vw_ref,
                     o_ref, m_ref, acc_ref):
    Q, KT, R = NSA_Q, NSA_KT, NSA_ROWS
    nsl = NSA_MAX_SLC
    bi = pl.program_id(0)
    q0 = bi * Q
    ncmp = kcmp_ref.shape[0]
    qs = jnp.concatenate([q_ref[:, h * HEAD_DIM:(h + 1) * HEAD_DIM] for h in range(N_HEADS)], axis=0)
    qpos_r = q0 + jnp.bitwise_and(lax.broadcasted_iota(jnp.int32, (R, 1), 0), Q - 1)

    last_vis = jnp.minimum(jnp.right_shift(qpos_r - (NSA_CMP_LEN - 1), 4), ncmp - 2)
    cmp_mask = lax.broadcasted_iota(jnp.int32, (R, ncmp), 1) <= last_vis
    s2 = jnp.where(cmp_mask, _dot_nt(qs, kcmp_ref[...]), MASK_VALUE)
    pc = jnp.exp2(s2 - jnp.max(s2, axis=-1, keepdims=True))
    inv = jnp.where(last_vis >= 0, 1.0 / jnp.sum(pc, axis=-1, keepdims=True), 0.0)
    pc = pc * inv
    o_cmp = _dot(pc.astype(BF16), vcmp_ref[...])

    W = NSA_WINDOW + Q
    ws = pl.multiple_of(jnp.maximum(q0 - NSA_WINDOW, 0), Q)
    dist = (qpos_r - ws) - lax.broadcasted_iota(jnp.int32, (R, W), 1)
    wmask = pltpu.bitcast(dist, jnp.uint32) < NSA_WINDOW
    s2 = jnp.where(wmask, _dot_nt(qs, kw_ref[pl.ds(ws, W), :]), MASK_VALUE)
    pw = jnp.exp2(s2 - jnp.max(s2, axis=-1, keepdims=True))
    ow = _dot(pw.astype(BF16), vw_ref[pl.ds(ws, W), :])
    o_win = ow[:, 0:HEAD_DIM] / ow[:, HEAD_DIM:]

    p_sum = pc[0:Q] + pc[Q:2 * Q] + pc[2 * Q:3 * Q] + pc[3 * Q:4 * Q]
    ph = p_sum.astype(BF16)
    plo = (p_sum - ph.astype(F32)).astype(BF16)
    imp_t = _dot_nt(ovt_ref[...], ph) + _dot_nt(ovt_ref[...], plo)
    blk = lax.broadcasted_iota(jnp.int32, (nsl, Q), 0)
    blkf = blk.astype(F32)
    qpos_l = q0 + lax.broadcasted_iota(jnp.int32, (nsl, Q), 1)
    cur = jnp.right_shift(qpos_l, NSA_SLC_SHIFT)
    forced = (blk == 0) | (blk == cur) | (blk == cur - 1)
    future = blk * NSA_SLC_LEN > qpos_l
    work = jnp.where(future, -1.0, jnp.where(forced, FORCED_SCORE, imp_t))
    work = jnp.where(blk < n_slc, work, -jnp.inf)
    sel = jnp.zeros((nsl, Q), F32)
    for _ in range(min(NSA_TOPK, n_slc)):
        mx = jnp.max(work, axis=0, keepdims=True)
        first = jnp.min(jnp.where(work == mx, blkf, float(nsl)), axis=0, keepdims=True)
        hit = blkf == first
        sel = jnp.where(hit, 1.0, sel)
        work = jnp.where(hit, -jnp.inf, work)
    bias_t = jnp.where(future, NSA_OFF, jnp.where(sel > 0.0, 0.0, NSA_OFF))
    bias = bias_t.T.astype(BF16)
    q_aug = jnp.concatenate([qs, jnp.concatenate([bias] * N_HEADS, axis=0)], axis=1)

    n_tiles = (q0 + Q + KT - 1) // KT

    def scores(t, width, causal):
        start = pl.multiple_of(t * KT, KT)
        s2 = _dot_nt(q_aug, ks_ref[pl.ds(start, width), :])
        if causal:
            kpos = start + lax.broadcasted_iota(jnp.int32, (R, width), 1)
            s2 = jnp.where(kpos <= qpos_r, s2, NSA_OFF)
        return start, s2

    def max_tile(t, width, causal):
        _, s2 = scores(t, width, causal)
        m = m_ref[...]
        for c in range(width // KT):
            m = jnp.maximum(m, _lanes4(s2[:, c * KT:(c + 1) * KT]))
        m_ref[...] = m

    def sum_tile(t, width, causal):
        start, s2 = scores(t, width, causal)
        pt = jnp.exp2(s2 - jnp.concatenate([m_ref[...]] * (width // LANES), axis=1))
        acc_ref[...] += _dot(pt.astype(BF16), vs_ref[pl.ds(start, width), :])

    def loop(fn):
        n_full = n_tiles - 1

        def body(u, carry):
            fn(2 * u, 2 * KT, False)
            return carry

        lax.fori_loop(0, n_full // 2, body, 0)

        @pl.when(n_full % 2 == 1)
        def _():
            fn(n_full - 1, KT, False)

        fn(n_tiles - 1, KT, True)

    m_ref[...] = jnp.full_like(m_ref, NSA_OFF)
    loop(max_tile)
    m_ref[...] = jnp.broadcast_to(jnp.max(m_ref[...], axis=-1, keepdims=True), m_ref.shape)
    acc_ref[...] = jnp.zeros_like(acc_ref)
    loop(sum_tile)
    o_slc = acc_ref[:, 0:HEAD_DIM] / acc_ref[:, HEAD_DIM:]

    gate = jax.nn.sigmoid(g_ref[...])
    for h in range(N_HEADS):
        rows = slice(h * Q, (h + 1) * Q)
        out = (gate[:, 3 * h:3 * h + 1] * o_cmp[rows] + gate[:, 3 * h + 1:3 * h + 2] * o_slc[rows]
               + gate[:, 3 * h + 2:3 * h + 3] * o_win[rows])
        o_ref[:, h * HEAD_DIM:(h + 1) * HEAD_DIM] = out.astype(o_ref.dtype)


def _nsa_overlap(S):
    n_cmp = (S - NSA_CMP_LEN) // NSA_CMP_STRIDE + 1
    n_slc = S // NSA_SLC_LEN
    nblk = S // NSA_CMP_STRIDE
    nsl = NSA_MAX_SLC
    assert n_slc <= nsl
    cmp_start = np.arange(n_cmp) * NSA_CMP_STRIDE
    slc_start = np.arange(n_slc) * NSA_SLC_LEN
    ov = (np.minimum(cmp_start[:, None] + NSA_CMP_LEN, slc_start[None, :] + NSA_SLC_LEN)
          - np.maximum(cmp_start[:, None], slc_start[None, :]))
    out = np.zeros((nblk, nsl), np.float32)
    out[:n_cmp, :n_slc] = np.clip(ov, 0, None) / NSA_CMP_LEN
    return out, n_slc


def _nsa_attention(qn16, p32, kcmp, vcmp, ks16, vs16, kw16, vw16, S):
    Q = NSA_Q
    ov, n_slc = _nsa_overlap(S)
    nblk, nsl = ov.shape
    full = lambda shape: pl.BlockSpec(shape, lambda i: (0,) * len(shape))
    return pl.pallas_call(
        functools.partial(_nsa_attn_kernel, n_slc),
        grid=(S // Q,),
        in_specs=[pl.BlockSpec((Q, GROUP_WIDTH), lambda i: (i, 0)),
                  pl.BlockSpec((Q, HEAD_DIM), lambda i: (i, COL_NG)),
                  full((nblk, HEAD_DIM)), full((nblk, HEAD_DIM)), full((nsl, nblk)),
                  full((S, HEAD_DIM + NSA_MAX_SLC)), full((S, 2 * HEAD_DIM)),
                  full((S, HEAD_DIM)), full((S, 2 * HEAD_DIM))],
        out_specs=pl.BlockSpec((Q, GROUP_WIDTH), lambda i: (i, 0)),
        out_shape=jax.ShapeDtypeStruct((S, GROUP_WIDTH), BF16),
        scratch_shapes=[pltpu.VMEM((NSA_ROWS, LANES), F32), pltpu.VMEM((NSA_ROWS, 2 * HEAD_DIM), F32)],
        compiler_params=_cparams(("arbitrary",), 48),
    )(qn16, p32, kcmp, vcmp, jnp.asarray(ov.T, BF16), ks16, vs16, kw16, vw16)


def _outproj_kernel(a0_ref, a1_ref, a2_ref, a3_ref, w_ref, x_ref, o_ref):
    acc = x_ref[...]
    for g, a_ref in enumerate((a0_ref, a1_ref, a2_ref, a3_ref)):
        acc = acc + _dot(a_ref[...], w_ref[g * GROUP_WIDTH:(g + 1) * GROUP_WIDTH, :])
    o_ref[...] = acc


def _outproj(parts, w16, layer, x, tm):
    S, N = x.shape
    a_spec = pl.BlockSpec((tm, GROUP_WIDTH), lambda i: (i, 0))
    return pl.pallas_call(
        _outproj_kernel,
        grid=(S // tm,),
        in_specs=[a_spec, a_spec, a_spec, a_spec,
                  _layer_spec(layer, (4 * GROUP_WIDTH, N)),
                  pl.BlockSpec((tm, N), lambda i: (i, 0))],
        out_specs=pl.BlockSpec((tm, N), lambda i: (i, 0)),
        out_shape=jax.ShapeDtypeStruct((S, N), F32),
        compiler_params=_cparams(("parallel",), 48),
    )(*parts, w16, x)


def _xattn_kernel(x_ref, g_ref, wq_ref, k_ref, v_ref, wo_ref, o_ref):
    x = x_ref[...]
    h = _rms(x, g_ref[...]).astype(BF16)
    q = _dot(h, wq_ref[...]).astype(BF16)
    outs = []
    for hd in range(N_HEADS):
        hs = slice(hd * HEAD_DIM, (hd + 1) * HEAD_DIM)
        s = _dot_nt(q[:, hs], k_ref[:, hs]) * ATTN_SCALE
        m = jnp.max(s, axis=-1, keepdims=True)
        p = jnp.exp(s - m)
        p = p / jnp.sum(p, axis=-1, keepdims=True)
        outs.append(_dot(p.astype(BF16), v_ref[:, hs]).astype(BF16))
    o = jnp.concatenate(outs, axis=-1)
    o_ref[...] = x + _dot(o, wo_ref[...])


def _xattn(x, gains, wq16, k16, v16, wo16, layer, tm):
    S, K = x.shape
    XW = wq16.shape[2]
    M = k16.shape[0]
    return pl.pallas_call(
        _xattn_kernel,
        grid=(S // tm,),
        in_specs=[pl.BlockSpec((tm, K), lambda i: (i, 0)),
                  _layer_spec(layer, (1, K)),
                  _layer_spec(layer, (K, XW)),
                  pl.BlockSpec((M, XW), lambda i: (0, 0)),
                  pl.BlockSpec((M, XW), lambda i: (0, 0)),
                  _layer_spec(layer, (XW, K))],
        out_specs=pl.BlockSpec((tm, K), lambda i: (i, 0)),
        out_shape=jax.ShapeDtypeStruct((S, K), F32),
        compiler_params=_cparams(("parallel",), 48),
    )(x, gains, wq16, k16, v16, wo16)


FFN_HALO = 16


FFN_SUB = 256


def _ffn_kernel(x_ref, xp_ref, g_ref, wg_ref, wv_ref, cwg_ref, cwv_ref, cbg_ref, cbv_ref, wd_ref,
                o_ref, h_ref):
    i = pl.program_id(0)
    f = pl.program_id(1)
    tf = wg_ref.shape[1]

    @pl.when(f == 0)
    def _():
        hp = _rms(xp_ref[...], g_ref[...])
        h_ref[0:FFN_HALO, :] = jnp.where(i > 0, hp, 0.0).astype(BF16)
        h_ref[FFN_HALO:, :] = _rms(x_ref[...], g_ref[...]).astype(BF16)
        o_ref[...] = x_ref[...]

    h = h_ref[...]

    def conv(w_ref, cw_ref, cb_ref, cols):
        u = _dot(h, w_ref[:, cols])
        u1 = pltpu.roll(u, 1, axis=0)
        u2 = pltpu.roll(u, 2, axis=0)
        c = cb_ref[:, cols] + cw_ref[2:3, cols] * u + cw_ref[1:2, cols] * u1 + cw_ref[0:1, cols] * u2
        return c[FFN_HALO:, :]

    acts = []
    for s in range(tf // FFN_SUB):
        cols = slice(s * FFN_SUB, (s + 1) * FFN_SUB)
        gate = conv(wg_ref, cwg_ref, cbg_ref, cols)
        val = conv(wv_ref, cwv_ref, cbv_ref, cols)
        acts.append((_silu(gate) * val).astype(BF16))
    o_ref[...] += _dot(jnp.concatenate(acts, axis=1), wd_ref[...])


def _ffn(x, gains, w_up16, conv_w, conv_b, w_down16, layer, tm, tf):
    S, K = x.shape
    F = w_down16.shape[1]
    nf = F // tf
    hb = tm // FFN_HALO
    return pl.pallas_call(
        _ffn_kernel,
        grid=(S // tm, nf),
        in_specs=[pl.BlockSpec((tm, K), lambda i, f: (i, 0), pipeline_mode=pl.Buffered(1)),
                  pl.BlockSpec((FFN_HALO, K), lambda i, f: (jnp.maximum(i * hb - 1, 0), 0)),
                  _layer_spec(layer, (1, K)),
                  _layer_spec(layer, (K, tf), lambda i, f: (0, f)),
                  _layer_spec(layer, (K, tf), lambda i, f: (0, nf + f)),
                  _layer_spec(layer, (3, tf), lambda i, f: (0, f)),
                  _layer_spec(layer, (3, tf), lambda i, f: (0, nf + f)),
                  _layer_spec(layer, (1, tf), lambda i, f: (0, f)),
                  _layer_spec(layer, (1, tf), lambda i, f: (0, nf + f)),
                  _layer_spec(layer, (tf, K), lambda i, f: (f, 0))],
        out_specs=pl.BlockSpec((tm, K), lambda i, f: (i, 0)),
        out_shape=jax.ShapeDtypeStruct((S, K), F32),
        scratch_shapes=[pltpu.VMEM((tm + FFN_HALO, K), BF16)],
        compiler_params=_cparams(("parallel", "arbitrary"), 56),
    )(x, x, gains, w_up16, w_up16, conv_w, conv_w, conv_b, conv_b, w_down16)


def _final_norm_kernel(x_ref, g_ref, o_ref):
    o_ref[...] = _rms(x_ref[...], g_ref[...])


def _final_norm(x, gain, tm):
    S, K = x.shape
    return pl.pallas_call(
        _final_norm_kernel,
        grid=(S // tm,),
        in_specs=[pl.BlockSpec((tm, K), lambda i: (i, 0)), pl.BlockSpec((1, K), lambda i: (0, 0))],
        out_specs=pl.BlockSpec((tm, K), lambda i: (i, 0)),
        out_shape=jax.ShapeDtypeStruct((S, K), F32),
        compiler_params=_cparams(("parallel",), 40),
    )(x, gain.reshape(1, K))


def kernel(x, mem, positions, mix_norm, w_in, ret_norm, hgrn_lb_logits, hgrn_norm, nsa_pos_k, nsa_pos_v,
           nsa_w_ck, nsa_w_cv, w_out, xattn_norm, mem_norm, xattn_wq, xattn_wk, xattn_wv, xattn_wo,
           ffn_norm, ffn_w_up, ffn_conv_w, ffn_conv_b, ffn_w_down, final_norm):
    B, S, _ = x.shape
    assert B == 1 and S % 1024 == 0
    L = DEPTH
    xs = x.reshape(S, D_MODEL)
    mem2 = mem.reshape(N_MEM, D_MODEL)
    row = lambda g: g.reshape(L, 1, g.shape[-1])
    w_in16 = jnp.pad(w_in.astype(BF16), ((0, 0), (0, 0), (0, IN_COLS_PAD - IN_COLS)))
    w_out16 = w_out.astype(BF16)
    wq16, wk16, wv16, wo16 = (w.astype(BF16) for w in (xattn_wq, xattn_wk, xattn_wv, xattn_wo))
    w_up16 = ffn_w_up.astype(BF16)
    w_down16 = ffn_w_down.astype(BF16)
    wck16 = nsa_w_ck.reshape(L, NSA_CMP_LEN * HEAD_DIM, HEAD_DIM).astype(BF16)
    wcv16 = nsa_w_cv.reshape(L, NSA_CMP_LEN * HEAD_DIM, HEAD_DIM).astype(BF16)
    pos_k = nsa_pos_k.reshape(L, 1, NSA_CMP_LEN * HEAD_DIM)
    pos_v = nsa_pos_v.reshape(L, 1, NSA_CMP_LEN * HEAD_DIM)
    mix_g, ret_g, hgrn_g, xattn_g, mem_g, ffn_g = (
        row(g) for g in (mix_norm, ret_norm, hgrn_norm, xattn_norm, mem_norm, ffn_norm))
    conv_b = row(ffn_conv_b)

    cos, sin = _rope_tables(positions, S)
    for layer in range(L):
        p32, p16 = _inproj(xs, mix_g, w_in16, layer, PROJ_TM, PROJ_TN)
        o_ret = _retention(p32, p16, cos, sin, ret_g, layer, S)
        o_sb = _stick_breaking(p16, S)
        o_hg = _hgrn2(p32, p16, hgrn_lb_logits, hgrn_g, layer, S)
        qn16, kc16, vc16, ks16, kw16, vs16, vw16 = _nsa_prep(p32, p16, cos, sin, S)
        kcmp, vcmp = _nsa_compress(kc16, vc16, wck16, wcv16, pos_k, pos_v, layer, S)
        o_nsa = _nsa_attention(qn16, p32, kcmp, vcmp, ks16, vs16, kw16, vw16, S)
        xs = _outproj((o_ret, o_sb, o_hg, o_nsa), w_out16, layer, xs, ROW_TM)
        k16 = _normproj(mem2, mem_g, wk16, layer, BF16)
        v16 = _normproj(mem2, mem_g, wv16, layer, BF16)
        xs = _xattn(xs, xattn_g, wq16, k16, v16, wo16, layer, ROW_TM)
        xs = _ffn(xs, ffn_g, w_up16, ffn_conv_w, conv_b, w_down16, layer, FFN_TM, FFN_TF)
    return _final_norm(xs, final_norm, ROW_TM).reshape(B, S, D_MODEL)
```

```python
import functools
import math

import numpy as np
import jax
import jax.numpy as jnp
from jax import lax
from jax.experimental import pallas as pl
from jax.experimental.pallas import tpu as pltpu

F32 = jnp.float32
BF16 = jnp.bfloat16

D_MODEL = 2048
DEPTH = 4
N_HEADS = 4
HEAD_DIM = 128
GROUP_WIDTH = N_HEADS * HEAD_DIM
ROPE_THETA = 10000.0
RET_CHUNK = 128
HGRN_CHUNK = 64
NSA_CMP_LEN = 32
NSA_CMP_STRIDE = 16
NSA_SLC_LEN = 64
NSA_SLC_SHIFT = 6
NSA_TOPK = 16
NSA_WINDOW = 512
N_MEM = 256
D_FF = 5632
NORM_EPS = 1e-6
MASK_VALUE = -1e30
FORCED_SCORE = 1e4
MIN_FORGET = 1e-6
IN_COLS = 12 * GROUP_WIDTH + 6 * HEAD_DIM + 3 * N_HEADS
ATTN_SCALE = HEAD_DIM ** -0.5
LANES = 128


def _in_cols_layout():
    G, D = GROUP_WIDTH, HEAD_DIM
    names = ["rq", "rk", "rv", "rg", "sq", "sk", "sv", "gq", "gf", "gi", "gg", "nq",
             "kc", "vc", "ks", "vs", "kw", "vw", "ng"]
    widths = [G] * 12 + [D] * 6 + [3 * N_HEADS]
    start = dict(zip(names, np.cumsum([0] + widths[:-1])))
    width = dict(zip(names, widths))
    f32_names = ["rq", "rk", "rg", "gq", "gf", "gg", "nq", "kc", "ks", "kw", "ng"]
    bf16_names = ["rv", "sq", "sk", "sv", "gi", "vc", "vs", "vw"]
    col, srcs = {}, []
    for group in (f32_names, bf16_names):
        pos, src = 0, []
        for n in group:
            assert pos % LANES == 0
            col[n] = pos // LANES
            src.append((int(start[n]), width[n]))
            pos += width[n]
        srcs.append(src)
    return srcs[0], srcs[1], col


_P32_SRC, _P16_SRC, _COL = _in_cols_layout()
P32_COLS = 4096
P16_COLS = 3072
COL_RQ, COL_RK, COL_RG = _COL["rq"], _COL["rk"], _COL["rg"]
COL_GQ, COL_GF, COL_GG, COL_NQ = _COL["gq"], _COL["gf"], _COL["gg"], _COL["nq"]
COL_KC, COL_KS, COL_KW, COL_NG = _COL["kc"], _COL["ks"], _COL["kw"], _COL["ng"]
COL_RV, COL_SQ, COL_SK, COL_SV, COL_GI = _COL["rv"], _COL["sq"], _COL["sk"], _COL["sv"], _COL["gi"]
COL_VC, COL_VS, COL_VW = _COL["vc"], _COL["vs"], _COL["vw"]

V7X_VMEM_BYTES = 64 * 1024 * 1024

PROJ_TM, PROJ_TN = 1024, 1024
ROW_TM = 512
FFN_TM, FFN_TF = 1024, 512


def _cparams(semantics, vmem_mb):
    return pltpu.CompilerParams(dimension_semantics=semantics,
                                vmem_limit_bytes=min(vmem_mb * 1024 * 1024, V7X_VMEM_BYTES - (8 << 20)))


def _layer_spec(layer, tail_shape, tail_index=None):
    if tail_index is None:
        tail_index = lambda *ids: (0,) * len(tail_shape)
    return pl.BlockSpec((None,) + tuple(tail_shape), lambda *ids: (layer,) + tuple(tail_index(*ids)))


def _dot(a, b):
    return jnp.dot(a, b, preferred_element_type=F32)


def _dot_nt(a, b):
    return lax.dot_general(a, b, (((1,), (1,)), ((), ())), preferred_element_type=F32)


def _dot_tn(a, b):
    return lax.dot_general(a, b, (((0,), (0,)), ((), ())), preferred_element_type=F32)


def _split3(x):
    h1 = x.astype(BF16)
    r1 = x - h1.astype(F32)
    h2 = r1.astype(BF16)
    r2 = r1 - h2.astype(F32)
    return h1, h2, r2.astype(BF16)


def _rms(x, gain):
    ms = jnp.mean(x * x, axis=-1, keepdims=True)
    return x * lax.rsqrt(ms + NORM_EPS) * gain


def _rope(x, cos, sin_signed):
    return x * cos + pltpu.roll(x, HEAD_DIM // 2, axis=1) * sin_signed


def _silu(x):
    return x * jax.nn.sigmoid(x)


def _rope_table_kernel(pos_ref, inv_ref, cos_ref, sin_ref):
    ang = pos_ref[...].astype(F32) * inv_ref[...]
    lane = lax.broadcasted_iota(jnp.int32, ang.shape, 1)
    cos_ref[...] = jnp.cos(ang)
    s = jnp.sin(ang)
    sin_ref[...] = jnp.where(lane < HEAD_DIM // 2, -s, s)


def _rope_tables(positions, S):
    half = HEAD_DIM // 2
    inv = ROPE_THETA ** (-jnp.arange(half, dtype=F32) / half)
    inv = jnp.concatenate([inv, inv]).reshape(1, HEAD_DIM)
    pos = positions.reshape(S, 1)
    tm = min(S, 1024)
    return pl.pallas_call(
        _rope_table_kernel,
        grid=(S // tm,),
        in_specs=[pl.BlockSpec((tm, 1), lambda i: (i, 0)),
                  pl.BlockSpec((1, HEAD_DIM), lambda i: (0, 0))],
        out_specs=[pl.BlockSpec((tm, HEAD_DIM), lambda i: (i, 0)),
                   pl.BlockSpec((tm, HEAD_DIM), lambda i: (i, 0))],
        out_shape=[jax.ShapeDtypeStruct((S, HEAD_DIM), F32)] * 2,
        compiler_params=_cparams(("arbitrary",), 32),
    )(pos, inv)


def _inproj_kernel(n32, x_ref, g_ref, w_ref, o32_ref, o16_ref, h_ref):
    j = pl.program_id(1)

    @pl.when(j == 0)
    def _():
        h_ref[...] = _rms(x_ref[...], g_ref[...]).astype(BF16)

    @pl.when(j < n32)
    def _():
        o32_ref[...] = _dot(h_ref[...], w_ref[...])

    @pl.when(j >= n32)
    def _():
        o16_ref[...] = _dot(h_ref[...], w_ref[...]).astype(BF16)


def _inproj(x, gains, w16, layer, tm, tn):
    S, K = x.shape
    n32, n16 = P32_COLS // tn, P16_COLS // tn
    assert w16.shape[2] == (n32 + n16) * tn
    return pl.pallas_call(
        functools.partial(_inproj_kernel, n32),
        grid=(S // tm, n32 + n16),
        in_specs=[pl.BlockSpec((tm, K), lambda i, j: (i, 0)),
                  _layer_spec(layer, (1, K)),
                  _layer_spec(layer, (K, tn), lambda i, j: (0, j))],
        out_specs=[pl.BlockSpec((tm, tn), lambda i, j: (i, jnp.minimum(j, n32 - 1))),
                   pl.BlockSpec((tm, tn), lambda i, j: (i, jnp.maximum(j - n32, 0)))],
        out_shape=[jax.ShapeDtypeStruct((S, P32_COLS), F32), jax.ShapeDtypeStruct((S, P16_COLS), BF16)],
        scratch_shapes=[pltpu.VMEM((tm, K), BF16)],
        compiler_params=_cparams(("parallel", "arbitrary"), 48),
    )(x, gains, w16)


def _regroup_w_in(w_in):
    def group(src, width):
        cols = [w_in[:, :, a:a + n] for a, n in src]
        pad = width - sum(n for _, n in src)
        if pad:
            cols.append(jnp.zeros(w_in.shape[:2] + (pad,), w_in.dtype))
        return cols

    return jnp.concatenate(group(_P32_SRC, P32_COLS) + group(_P16_SRC, P16_COLS), axis=-1).astype(BF16)


def _normproj_kernel(x_ref, g_ref, w_ref, o_ref):
    h = _rms(x_ref[...], g_ref[...]).astype(BF16)
    o_ref[...] = _dot(h, w_ref[...]).astype(o_ref.dtype)


def _normproj(x, gains, w16, layer, out_dtype):
    M, K = x.shape
    N = w16.shape[2]
    return pl.pallas_call(
        _normproj_kernel,
        grid=(1,),
        in_specs=[pl.BlockSpec((M, K), lambda i: (0, 0)),
                  _layer_spec(layer, (1, K)),
                  _layer_spec(layer, (K, N))],
        out_specs=pl.BlockSpec((M, N), lambda i: (0, 0)),
        out_shape=jax.ShapeDtypeStruct((M, N), out_dtype),
        compiler_params=_cparams(("arbitrary",), 32),
    )(x, gains, w16)


_RET_LOG_GAMMA = [math.log1p(-(2.0 ** (-5.0 - h))) for h in range(N_HEADS)]


def _ret_kernel(q_ref, k_ref, v_ref, g_ref, cos_ref, sin_ref, gain_ref, o_ref, st_ref):
    C = RET_CHUNK

    @pl.when(pl.program_id(0) == 0)
    def _():
        st_ref[...] = jnp.zeros_like(st_ref)

    cos = cos_ref[...]
    sin = sin_ref[...]
    row = lax.broadcasted_iota(jnp.int32, (C, C), 0)
    col = lax.broadcasted_iota(jnp.int32, (C, C), 1)
    rel = (row - col).astype(F32)
    rowf = row.astype(F32)
    for h in range(N_HEADS):
        hs = slice(h * HEAD_DIM, (h + 1) * HEAD_DIM)
        lg = _RET_LOG_GAMMA[h]
        decay = jnp.where(rel >= 0, jnp.exp(lg * jnp.maximum(rel, 0.0)), 0.0)
        q = _rope(q_ref[:, hs], cos, sin)
        k = _rope(k_ref[:, hs], cos, sin) * ATTN_SCALE
        v = v_ref[:, hs]
        st = st_ref[h]
        s = _dot_nt(q.astype(BF16), k.astype(BF16)) * decay
        o = _dot(s.astype(BF16), v)
        q_w = jnp.exp(lg * (rowf + 1.0))
        o = o + _dot_nt((q * q_w).astype(BF16), st.astype(BF16))
        k_w = jnp.exp(lg * (C - 1.0 - rowf))
        st_ref[h] = st * math.exp(lg * C) + _dot_tn(v, (k * k_w).astype(BF16))
        oc = o - jnp.mean(o, axis=-1, keepdims=True)
        y = oc * lax.rsqrt(jnp.mean(oc * oc, axis=-1, keepdims=True) + NORM_EPS) * gain_ref[:, hs]
        o_ref[:, hs] = (y * _silu(g_ref[:, hs])).astype(o_ref.dtype)


def _retention(p32, p16, cos, sin, gains, layer, S):
    C = RET_CHUNK
    gw = GROUP_WIDTH // LANES
    return pl.pallas_call(
        _ret_kernel,
        grid=(S // C,),
        in_specs=[pl.BlockSpec((C, GROUP_WIDTH), lambda c: (c, COL_RQ // gw)),
                  pl.BlockSpec((C, GROUP_WIDTH), lambda c: (c, COL_RK // gw)),
                  pl.BlockSpec((C, GROUP_WIDTH), lambda c: (c, COL_RV // gw)),
                  pl.BlockSpec((C, GROUP_WIDTH), lambda c: (c, COL_RG // gw)),
                  pl.BlockSpec((C, HEAD_DIM), lambda c: (c, 0)),
                  pl.BlockSpec((C, HEAD_DIM), lambda c: (c, 0)),
                  _layer_spec(layer, (1, GROUP_WIDTH))],
        out_specs=pl.BlockSpec((C, GROUP_WIDTH), lambda c: (c, 0)),
        out_shape=jax.ShapeDtypeStruct((S, GROUP_WIDTH), BF16),
        scratch_shapes=[pltpu.VMEM((N_HEADS, HEAD_DIM, HEAD_DIM), F32)],
        compiler_params=_cparams(("arbitrary",), 32),
    )(p32, p32, p16, p32, cos, sin, gains)


SB_TILE = 256
SB_DEAD_LOG = -104.0


def _sb_tile(q, k_ref, v_ref, j, run, acc, upper, diag, on=None):
    T = SB_TILE
    start = pl.multiple_of(j * T, T)
    k = k_ref[pl.ds(start, T), :]
    v = v_ref[pl.ds(start, T), :]
    z = _dot_nt(q, k) * ATTN_SCALE
    sp = jnp.maximum(z, 0.0) + jnp.log(1.0 + jnp.exp(-jnp.abs(z)))
    log_1m = -sp
    strict = on
    if diag:
        row = lax.broadcasted_iota(jnp.int32, (T, T), 0)
        col = lax.broadcasted_iota(jnp.int32, (T, T), 1)
        strict = col < row
    if strict is not None:
        log_1m = jnp.where(strict, log_1m, 0.0)
    hi = log_1m.astype(BF16)
    lo = (log_1m - hi.astype(F32)).astype(BF16)
    both = _dot(jnp.concatenate([hi, lo], axis=0), upper)
    between = both[:T] + both[T:] + run
    w = jnp.exp((z - sp) + between)
    if strict is not None:
        w = jnp.where(strict, w, 0.0)
    acc = acc + _dot(w.astype(BF16), v)
    run = between[:, 0:1] + log_1m[:, 0:1]
    return run, acc


def _sb_kernel(q_ref, k_ref, v_ref, o_ref):
    T = SB_TILE
    i = pl.program_id(1)
    q = q_ref[...]
    r = lax.broadcasted_iota(jnp.int32, (T, T), 0)
    c = lax.broadcasted_iota(jnp.int32, (T, T), 1)
    upper = jnp.where(r > c, 1.0, 0.0).astype(BF16)
    run = jnp.zeros((T, 1), F32)
    acc = jnp.zeros((T, HEAD_DIM), F32)
    run, acc = _sb_tile(q, k_ref, v_ref, i, run, acc, upper, True)
    run, acc = _sb_tile(q, k_ref, v_ref, jnp.maximum(i - 1, 0), run, acc, upper, False, on=i > 0)

    def alive(run):
        return (jnp.max(run) > SB_DEAD_LOG).astype(jnp.int32)

    def cond(carry):
        step, live, _, _ = carry
        return jnp.logical_and(step < i, live > 0)

    def body(carry):
        step, _, run, acc = carry
        run, acc = _sb_tile(q, k_ref, v_ref, i - 1 - step, run, acc, upper, False)
        return step + 1, alive(run), run, acc

    _, _, _, acc = lax.while_loop(cond, body, (jnp.int32(1), alive(run), run, acc))
    o_ref[...] = acc.astype(o_ref.dtype)


def _stick_breaking(p16, S):
    T = SB_TILE
    return pl.pallas_call(
        _sb_kernel,
        grid=(N_HEADS, S // T),
        in_specs=[pl.BlockSpec((T, HEAD_DIM), lambda h, i: (i, COL_SQ + h)),
                  pl.BlockSpec((S, HEAD_DIM), lambda h, i: (0, COL_SK + h)),
                  pl.BlockSpec((S, HEAD_DIM), lambda h, i: (0, COL_SV + h))],
        out_specs=pl.BlockSpec((T, HEAD_DIM), lambda h, i: (i, h)),
        out_shape=jax.ShapeDtypeStruct((S, GROUP_WIDTH), BF16),
        compiler_params=_cparams(("parallel", "arbitrary"), 40),
    )(p16, p16, p16)


_HG_LEVELS = (32, 16, 8, 4, 2, 1)
HGRN_STEP = 2


def _hgrn_constants():
    C = HGRN_CHUNK
    n = np.arange(C)
    seg = np.zeros((8 * C, C), np.float32)
    pair = np.zeros((7, C, C), np.float32)
    for l, s in enumerate(_HG_LEVELS):
        blk = n // s
        odd = (blk % 2) == 1
        a = blk * s
        e = a + s - 1
        j = n[None, :]
        t_odd = (j >= a[:, None]) & (j <= n[:, None])
        t_even = (j > n[:, None]) & (j <= e[:, None])
        seg[l * C:(l + 1) * C] = np.where(odd[:, None], t_odd, t_even)
        pair[l] = odd[:, None] & (blk[None, :] == blk[:, None] - 1)
    seg[6 * C:7 * C] = n[None, :] <= n[:, None]
    seg[7 * C:8 * C] = n[None, :] > n[:, None]
    pair[6] = np.eye(C)
    return seg, pair


def _hgrn_kernel(layer, q_ref, f_ref, v_ref, g_ref, lbl_ref, gain_ref, seg_ref, pair_ref, o_ref, st_ref):
    C = HGRN_CHUNK

    @pl.when(pl.program_id(0) == 0)
    def _():
        st_ref[...] = jnp.zeros_like(st_ref)

    logits = lbl_ref[...]
    e = jnp.exp(logits - jnp.max(logits, axis=0, keepdims=True))
    p = e / jnp.sum(e, axis=0, keepdims=True)
    lb = jnp.zeros((1, GROUP_WIDTH), F32)
    for l in range(1, layer + 1):
        lb = lb + p[l:l + 1, :]

    f = lb + (1.0 - lb) * jax.nn.sigmoid(f_ref[...])
    log_f = jnp.log(jnp.maximum(f, MIN_FORGET))
    kk = 1.0 - f
    parts = _split3(log_f)
    zero = jnp.zeros((C, GROUP_WIDTH), BF16)
    rhs = jnp.concatenate(
        [jnp.concatenate([p[c * C:(c + 1) * C] for p in parts] + [zero], axis=0) for c in range(HGRN_STEP)],
        axis=1)
    ex = jnp.exp(_dot(seg_ref[...], rhs))

    for h in range(N_HEADS):
        hs = slice(h * HEAD_DIM, (h + 1) * HEAD_DIM)
        st = st_ref[h]
        for c in range(HGRN_STEP):
            rs = slice(c * C, (c + 1) * C)
            es = slice(c * GROUP_WIDTH + h * HEAD_DIM, c * GROUP_WIDTH + (h + 1) * HEAD_DIM)
            q = q_ref[rs, hs] * ATTN_SCALE
            k = kk[rs, hs]
            v = v_ref[rs, hs]
            s = pair_ref[6] * _dot_nt(q.astype(BF16), k.astype(BF16))
            for l in range(6):
                el = ex[l * C:(l + 1) * C, es]
                s = s + pair_ref[l] * _dot_nt((q * el).astype(BF16), (k * el).astype(BF16))
            e_cum = ex[6 * C:7 * C, es]
            e_rest = ex[7 * C:8 * C, es]
            o = _dot(s.astype(BF16), v) + _dot_nt((q * e_cum).astype(BF16), st.astype(BF16))
            st = st * e_cum[C - 1:C, :] + _dot_tn(v, (k * e_rest).astype(BF16))
            y = o * lax.rsqrt(jnp.mean(o * o, axis=-1, keepdims=True) + NORM_EPS) * gain_ref[:, hs]
            o_ref[rs, hs] = (y * _silu(g_ref[rs, hs])).astype(o_ref.dtype)
        st_ref[h] = st


def _hgrn2(p32, p16, lb_logits, gains, layer, S):
    C = HGRN_CHUNK
    rows = C * HGRN_STEP
    gw = GROUP_WIDTH // LANES
    seg, pair = _hgrn_constants()
    seg = np.concatenate([seg, seg, seg, np.zeros_like(seg)], axis=1)
    return pl.pallas_call(
        functools.partial(_hgrn_kernel, layer),
        grid=(S // rows,),
        in_specs=[pl.BlockSpec((rows, GROUP_WIDTH), lambda c: (c, COL_GQ // gw)),
                  pl.BlockSpec((rows, GROUP_WIDTH), lambda c: (c, COL_GF // gw)),
                  pl.BlockSpec((rows, GROUP_WIDTH), lambda c: (c, COL_GI // gw)),
                  pl.BlockSpec((rows, GROUP_WIDTH), lambda c: (c, COL_GG // gw)),
                  pl.BlockSpec((DEPTH, GROUP_WIDTH), lambda c: (0, 0)),
                  _layer_spec(layer, (1, GROUP_WIDTH)),
                  pl.BlockSpec((8 * C, 4 * C), lambda c: (0, 0)),
                  pl.BlockSpec((7, C, C), lambda c: (0, 0, 0))],
        out_specs=pl.BlockSpec((rows, GROUP_WIDTH), lambda c: (c, 0)),
        out_shape=jax.ShapeDtypeStruct((S, GROUP_WIDTH), BF16),
        scratch_shapes=[pltpu.VMEM((N_HEADS, HEAD_DIM, HEAD_DIM), F32)],
        compiler_params=_cparams(("arbitrary",), 32),
    )(p32, p32, p16, p32, lb_logits, gains,
      jnp.asarray(seg, BF16), jnp.asarray(pair, F32))


NSA_Q_SCALE = ATTN_SCALE * math.log2(math.e)
NSA_MAX_SLC = LANES


def _nsa_prep_kernel(q_ref, kc_ref, vc_ref, ks_ref, kw_ref, vs_ref, vw_ref, cos_ref, sin_ref,
                     qo_ref, kco_ref, vco_ref, kso_ref, kwo_ref, vso_ref, vwo_ref):
    tm = q_ref.shape[0]
    ones = jnp.ones((tm, HEAD_DIM), BF16)
    vso_ref[:, 0:HEAD_DIM] = vs_ref[...]
    vso_ref[:, HEAD_DIM:] = ones
    vwo_ref[:, 0:HEAD_DIM] = vw_ref[...]
    vwo_ref[:, HEAD_DIM:] = ones
    cos = cos_ref[...]
    sin = sin_ref[...]
    for h in range(N_HEADS):
        hs = slice(h * HEAD_DIM, (h + 1) * HEAD_DIM)
        qo_ref[:, hs] = (_rope(q_ref[:, hs], cos, sin) * NSA_Q_SCALE).astype(BF16)
    kco_ref[...] = _rope(kc_ref[...], cos, sin).astype(BF16)
    kwo_ref[...] = _rope(kw_ref[...], cos, sin).astype(BF16)
    vco_ref[...] = vc_ref[...]
    kso_ref[:, 0:HEAD_DIM] = _rope(ks_ref[...], cos, sin).astype(BF16)
    tok = pl.program_id(0) * tm + lax.broadcasted_iota(jnp.int32, (tm, NSA_MAX_SLC), 0)
    lane = lax.broadcasted_iota(jnp.int32, (tm, NSA_MAX_SLC), 1)
    kso_ref[:, HEAD_DIM:] = jnp.where(jnp.right_shift(tok, NSA_SLC_SHIFT) == lane, 1.0, 0.0).astype(BF16)


def _nsa_prep(p32, p16, cos, sin, S):
    tm = min(S, 512)
    gw = GROUP_WIDTH // LANES
    row = lambda col: pl.BlockSpec((tm, HEAD_DIM), lambda i: (i, col))
    out = pl.BlockSpec((tm, HEAD_DIM), lambda i: (i, 0))
    aug = pl.BlockSpec((tm, HEAD_DIM + NSA_MAX_SLC), lambda i: (i, 0))
    return pl.pallas_call(
        _nsa_prep_kernel,
        grid=(S // tm,),
        in_specs=[pl.BlockSpec((tm, GROUP_WIDTH), lambda i: (i, COL_NQ // gw)),
                  row(COL_KC), row(COL_VC), row(COL_KS), row(COL_KW), row(COL_VS), row(COL_VW), out, out],
        out_specs=[pl.BlockSpec((tm, GROUP_WIDTH), lambda i: (i, 0)), out, out, aug, out, aug, aug],
        out_shape=[jax.ShapeDtypeStruct((S, GROUP_WIDTH), BF16),
                   jax.ShapeDtypeStruct((S, HEAD_DIM), BF16),
                   jax.ShapeDtypeStruct((S, HEAD_DIM), BF16),
                   jax.ShapeDtypeStruct((S, HEAD_DIM + NSA_MAX_SLC), BF16),
                   jax.ShapeDtypeStruct((S, HEAD_DIM), BF16),
                   jax.ShapeDtypeStruct((S, 2 * HEAD_DIM), BF16),
                   jax.ShapeDtypeStruct((S, 2 * HEAD_DIM), BF16)],
        compiler_params=_cparams(("parallel",), 32),
    )(p32, p32, p16, p32, p32, p16, p16, cos, sin)


def _nsa_compress_kernel(kb_ref, vb_ref, wk_ref, wv_ref, pk_ref, pv_ref, ko_ref, vo_ref):
    nblk = kb_ref.shape[0]
    half = NSA_CMP_STRIDE * HEAD_DIM
    rowid = lax.broadcasted_iota(jnp.int32, (nblk, HEAD_DIM), 0)
    for b_ref, w_ref, p_ref, o_ref in ((kb_ref, wk_ref, pk_ref, ko_ref), (vb_ref, wv_ref, pv_ref, vo_ref)):
        w = w_ref[...]
        top = _dot(b_ref[...], w[:half])
        bot = _dot(b_ref[...], w[half:])
        ph, pm, plo = _split3(p_ref[...])
        const = _dot(ph, w) + _dot(pm, w) + _dot(plo, w)
        shifted = pltpu.roll(bot, nblk - 1, axis=0)
        out = top + shifted + const
        o_ref[...] = jnp.where(rowid < nblk - 1, out, 0.0).astype(BF16)


def _nsa_compress(kc16, vc16, wk16, wv16, pos_k, pos_v, layer, S):
    nblk = S // NSA_CMP_STRIDE
    width = NSA_CMP_STRIDE * HEAD_DIM
    kb = kc16.reshape(nblk, width)
    vb = vc16.reshape(nblk, width)
    full = lambda shape: pl.BlockSpec(shape, lambda i: (0,) * len(shape))
    return pl.pallas_call(
        _nsa_compress_kernel,
        grid=(1,),
        in_specs=[full((nblk, width)), full((nblk, width)),
                  _layer_spec(layer, (2 * width, HEAD_DIM)), _layer_spec(layer, (2 * width, HEAD_DIM)),
                  _layer_spec(layer, (1, 2 * width)), _layer_spec(layer, (1, 2 * width))],
        out_specs=[full((nblk, HEAD_DIM)), full((nblk, HEAD_DIM))],
        out_shape=[jax.ShapeDtypeStruct((nblk, HEAD_DIM), BF16)] * 2,
        compiler_params=_cparams(("arbitrary",), 32),
    )(kb, vb, wk16, wv16, pos_k, pos_v)


NSA_Q = 256
NSA_KT = 512
NSA_ROWS = N_HEADS * NSA_Q
NSA_OFF = -(2.0 ** 126)


def _lanes4(x):
    return jnp.maximum(jnp.maximum(x[:, 0:LANES], x[:, LANES:2 * LANES]),
                       jnp.maximum(x[:, 2 * LANES:3 * LANES], x[:, 3 * LANES:4 * LANES]))


def _nsa_attn_kernel(n_slc, q_ref, g_ref, kcmp_ref, vcmp_ref, ovt_ref, ks_ref, vs_ref, kw_ref, vw_ref,
                     o_ref, m_ref, acc_ref):
    Q, KT, R = NSA_Q, NSA_KT, NSA_ROWS
    nsl = NSA_MAX_SLC
    bi = pl.program_id(0)
    q0 = bi * Q
    ncmp = kcmp_ref.shape[0]
    qs = jnp.concatenate([q_ref[:, h * HEAD_DIM:(h + 1) * HEAD_DIM] for h in range(N_HEADS)], axis=0)
    qpos_r = q0 + jnp.bitwise_and(lax.broadcasted_iota(jnp.int32, (R, 1), 0), Q - 1)

    last_vis = jnp.minimum(jnp.right_shift(qpos_r - (NSA_CMP_LEN - 1), 4), ncmp - 2)
    cmp_mask = lax.broadcasted_iota(jnp.int32, (R, ncmp), 1) <= last_vis
    s2 = jnp.where(cmp_mask, _dot_nt(qs, kcmp_ref[...]), MASK_VALUE)
    pc = jnp.exp2(s2 - jnp.max(s2, axis=-1, keepdims=True))
    inv = jnp.where(last_vis >= 0, 1.0 / jnp.sum(pc, axis=-1, keepdims=True), 0.0)
    pc = pc * inv
    o_cmp = _dot(pc.astype(BF16), vcmp_ref[...])

    W = NSA_WINDOW + Q
    ws = pl.multiple_of(jnp.maximum(q0 - NSA_WINDOW, 0), Q)
    dist = (qpos_r - ws) - lax.broadcasted_iota(jnp.int32, (R, W), 1)
    wmask = pltpu.bitcast(dist, jnp.uint32) < NSA_WINDOW
    s2 = jnp.where(wmask, _dot_nt(qs, kw_ref[pl.ds(ws, W), :]), MASK_VALUE)
    pw = jnp.exp2(s2 - jnp.max(s2, axis=-1, keepdims=True))
    ow = _dot(pw.astype(BF16), vw_ref[pl.ds(ws, W), :])
    o_win = ow[:, 0:HEAD_DIM] / ow[:, HEAD_DIM:]

    p_sum = pc[0:Q] + pc[Q:2 * Q] + pc[2 * Q:3 * Q] + pc[3 * Q:4 * Q]
    ph = p_sum.astype(BF16)
    plo = (p_sum - ph.astype(F32)).astype(BF16)
    imp_t = _dot_nt(ovt_ref[...], ph) + _dot_nt(ovt_ref[...], plo)
    blk = lax.broadcasted_iota(jnp.int32, (nsl, Q), 0)
    blkf = blk.astype(F32)
    qpos_l = q0 + lax.broadcasted_iota(jnp.int32, (nsl, Q), 1)
    cur = jnp.right_shift(qpos_l, NSA_SLC_SHIFT)
    forced = (blk == 0) | (blk == cur) | (blk == cur - 1)
    future = blk * NSA_SLC_LEN > qpos_l
    work = jnp.where(future, -1.0, jnp.where(forced, FORCED_SCORE, imp_t))
    work = jnp.where(blk < n_slc, work, -jnp.inf)
    sel = jnp.zeros((nsl, Q), F32)
    for _ in range(min(NSA_TOPK, n_slc)):
        mx = jnp.max(work, axis=0, keepdims=True)
        first = jnp.min(jnp.where(work == mx, blkf, float(nsl)), axis=0, keepdims=True)
        hit = blkf == first
        sel = jnp.where(hit, 1.0, sel)
        work = jnp.where(hit, -jnp.inf, work)
    bias_t = jnp.where(future, NSA_OFF, jnp.where(sel > 0.0, 0.0, NSA_OFF))
    bias = bias_t.T.astype(BF16)
    q_aug = jnp.concatenate([qs, jnp.concatenate([bias] * N_HEADS, axis=0)], axis=1)

    n_tiles = (q0 + Q + KT - 1) // KT

    def scores(t, width, causal):
        start = pl.multiple_of(t * KT, KT)
        s2 = _dot_nt(q_aug, ks_ref[pl.ds(start, width), :])
        if causal:
            kpos = start + lax.broadcasted_iota(jnp.int32, (R, width), 1)
            s2 = jnp.where(kpos <= qpos_r, s2, NSA_OFF)
        return start, s2

    def max_tile(t, width, causal):
        _, s2 = scores(t, width, causal)
        m = m_ref[...]
        for c in range(width // KT):
            m = jnp.maximum(m, _lanes4(s2[:, c * KT:(c + 1) * KT]))
        m_ref[...] = m

    def sum_tile(t, width, causal):
        start, s2 = scores(t, width, causal)
        pt = jnp.exp2(s2 - jnp.concatenate([m_ref[...]] * (width // LANES), axis=1))
        acc_ref[...] += _dot(pt.astype(BF16), vs_ref[pl.ds(start, width), :])

    def loop(fn):
        n_full = n_tiles - 1

        def body(u, carry):
            fn(2 * u, 2 * KT, False)
            return carry

        lax.fori_loop(0, n_full // 2, body, 0)

        @pl.when(n_full % 2 == 1)
        def _():
            fn(n_full - 1, KT, False)

        fn(n_tiles - 1, KT, True)

    m_ref[...] = jnp.full_like(m_ref, NSA_OFF)
    loop(max_tile)
    m_ref[...] = jnp.broadcast_to(jnp.max(m_ref[...], axis=-1, keepdims=True), m_ref.shape)
    acc_ref[...] = jnp.zeros_like(acc_ref)
    loop(sum_tile)
    o_slc = acc_ref[:, 0:HEAD_DIM] / acc_ref[:, HEAD_DIM:]

    gate = jax.nn.sigmoid(g_ref[...])
    for h in range(N_HEADS):
        rows = slice(h * Q, (h + 1) * Q)
        out = (gate[:, 3 * h:3 * h + 1] * o_cmp[rows] + gate[:, 3 * h + 1:3 * h + 2] * o_slc[rows]
               + gate[:, 3 * h + 2:3 * h + 3] * o_win[rows])
        o_ref[:, h * HEAD_DIM:(h + 1) * HEAD_DIM] = out.astype(o_ref.dtype)


def _nsa_overlap(S):
    n_cmp = (S - NSA_CMP_LEN) // NSA_CMP_STRIDE + 1
    n_slc = S // NSA_SLC_LEN
    nblk = S // NSA_CMP_STRIDE
    nsl = NSA_MAX_SLC
    assert n_slc <= nsl
    cmp_start = np.arange(n_cmp) * NSA_CMP_STRIDE
    slc_start = np.arange(n_slc) * NSA_SLC_LEN
    ov = (np.minimum(cmp_start[:, None] + NSA_CMP_LEN, slc_start[None, :] + NSA_SLC_LEN)
          - np.maximum(cmp_start[:, None], slc_start[None, :]))
    out = np.zeros((nblk, nsl), np.float32)
    out[:n_cmp, :n_slc] = np.clip(ov, 0, None) / NSA_CMP_LEN
    return out, n_slc


def _nsa_attention(qn16, p32, kcmp, vcmp, ks16, vs16, kw16, vw16, S):
    Q = NSA_Q
    ov, n_slc = _nsa_overlap(S)
    nblk, nsl = ov.shape
    full = lambda shape: pl.BlockSpec(shape, lambda i: (0,) * len(shape))
    return pl.pallas_call(
        functools.partial(_nsa_attn_kernel, n_slc),
        grid=(S // Q,),
        in_specs=[pl.BlockSpec((Q, GROUP_WIDTH), lambda i: (i, 0)),
                  pl.BlockSpec((Q, HEAD_DIM), lambda i: (i, COL_NG)),
                  full((nblk, HEAD_DIM)), full((nblk, HEAD_DIM)), full((nsl, nblk)),
                  full((S, HEAD_DIM + NSA_MAX_SLC)), full((S, 2 * HEAD_DIM)),
                  full((S, HEAD_DIM)), full((S, 2 * HEAD_DIM))],
        out_specs=pl.BlockSpec((Q, GROUP_WIDTH), lambda i: (i, 0)),
        out_shape=jax.ShapeDtypeStruct((S, GROUP_WIDTH), BF16),
        scratch_shapes=[pltpu.VMEM((NSA_ROWS, LANES), F32), pltpu.VMEM((NSA_ROWS, 2 * HEAD_DIM), F32)],
        compiler_params=_cparams(("arbitrary",), 48),
    )(qn16, p32, kcmp, vcmp, jnp.asarray(ov.T, BF16), ks16, vs16, kw16, vw16)


def _outproj_kernel(a0_ref, a1_ref, a2_ref, a3_ref, w_ref, x_ref, o_ref):
    acc = x_ref[...]
    for g, a_ref in enumerate((a0_ref, a1_ref, a2_ref, a3_ref)):
        acc = acc + _dot(a_ref[...], w_ref[g * GROUP_WIDTH:(g + 1) * GROUP_WIDTH, :])
    o_ref[...] = acc


def _outproj(parts, w16, layer, x, tm):
    S, N = x.shape
    a_spec = pl.BlockSpec((tm, GROUP_WIDTH), lambda i: (i, 0))
    return pl.pallas_call(
        _outproj_kernel,
        grid=(S // tm,),
        in_specs=[a_spec, a_spec, a_spec, a_spec,
                  _layer_spec(layer, (4 * GROUP_WIDTH, N)),
                  pl.BlockSpec((tm, N), lambda i: (i, 0))],
        out_specs=pl.BlockSpec((tm, N), lambda i: (i, 0)),
        out_shape=jax.ShapeDtypeStruct((S, N), F32),
        compiler_params=_cparams(("parallel",), 48),
    )(*parts, w16, x)


def _xattn_kernel(x_ref, g_ref, wq_ref, k_ref, v_ref, wo_ref, o_ref):
    x = x_ref[...]
    h = _rms(x, g_ref[...]).astype(BF16)
    q = _dot(h, wq_ref[...]).astype(BF16)
    outs = []
    for hd in range(N_HEADS):
        hs = slice(hd * HEAD_DIM, (hd + 1) * HEAD_DIM)
        s = _dot_nt(q[:, hs], k_ref[:, hs]) * ATTN_SCALE
        m = jnp.max(s, axis=-1, keepdims=True)
        p = jnp.exp(s - m)
        p = p / jnp.sum(p, axis=-1, keepdims=True)
        outs.append(_dot(p.astype(BF16), v_ref[:, hs]).astype(BF16))
    o = jnp.concatenate(outs, axis=-1)
    o_ref[...] = x + _dot(o, wo_ref[...])


def _xattn(x, gains, wq16, k16, v16, wo16, layer, tm):
    S, K = x.shape
    XW = wq16.shape[2]
    M = k16.shape[0]
    return pl.pallas_call(
        _xattn_kernel,
        grid=(S // tm,),
        in_specs=[pl.BlockSpec((tm, K), lambda i: (i, 0)),
                  _layer_spec(layer, (1, K)),
                  _layer_spec(layer, (K, XW)),
                  pl.BlockSpec((M, XW), lambda i: (0, 0)),
                  pl.BlockSpec((M, XW), lambda i: (0, 0)),
                  _layer_spec(layer, (XW, K))],
        out_specs=pl.BlockSpec((tm, K), lambda i: (i, 0)),
        out_shape=jax.ShapeDtypeStruct((S, K), F32),
        compiler_params=_cparams(("parallel",), 48),
    )(x, gains, wq16, k16, v16, wo16)


FFN_HALO = 16
FFN_SUB = 256


def _ffn_kernel(x_ref, xp_ref, g_ref, wg_ref, wv_ref, cwg_ref, cwv_ref, cbg_ref, cbv_ref, wd_ref,
                o_ref, h_ref):
    i = pl.program_id(0)
    f = pl.program_id(1)
    tf = wg_ref.shape[1]

    @pl.when(f == 0)
    def _():
        hp = _rms(xp_ref[...], g_ref[...])
        h_ref[0:FFN_HALO, :] = jnp.where(i > 0, hp, 0.0).astype(BF16)
        h_ref[FFN_HALO:, :] = _rms(x_ref[...], g_ref[...]).astype(BF16)
        o_ref[...] = x_ref[...]

    h = h_ref[...]

    def conv(w_ref, cw_ref, cb_ref, cols):
        u = _dot(h, w_ref[:, cols])
        u1 = pltpu.roll(u, 1, axis=0)
        u2 = pltpu.roll(u, 2, axis=0)
        c = cb_ref[:, cols] + cw_ref[2:3, cols] * u + cw_ref[1:2, cols] * u1 + cw_ref[0:1, cols] * u2
        return c[FFN_HALO:, :]

    acts = []
    for s in range(tf // FFN_SUB):
        cols = slice(s * FFN_SUB, (s + 1) * FFN_SUB)
        gate = conv(wg_ref, cwg_ref, cbg_ref, cols)
        val = conv(wv_ref, cwv_ref, cbv_ref, cols)
        acts.append((_silu(gate) * val).astype(BF16))
    o_ref[...] += _dot(jnp.concatenate(acts, axis=1), wd_ref[...])


def _ffn(x, gains, w_up16, conv_w, conv_b, w_down16, layer, tm, tf):
    S, K = x.shape
    F = w_down16.shape[1]
    nf = F // tf
    hb = tm // FFN_HALO
    return pl.pallas_call(
        _ffn_kernel,
        grid=(S // tm, nf),
        in_specs=[pl.BlockSpec((tm, K), lambda i, f: (i, 0), pipeline_mode=pl.Buffered(1)),
                  pl.BlockSpec((FFN_HALO, K), lambda i, f: (jnp.maximum(i * hb - 1, 0), 0)),
                  _layer_spec(layer, (1, K)),
                  _layer_spec(layer, (K, tf), lambda i, f: (0, f)),
                  _layer_spec(layer, (K, tf), lambda i, f: (0, nf + f)),
                  _layer_spec(layer, (3, tf), lambda i, f: (0, f)),
                  _layer_spec(layer, (3, tf), lambda i, f: (0, nf + f)),
                  _layer_spec(layer, (1, tf), lambda i, f: (0, f)),
                  _layer_spec(layer, (1, tf), lambda i, f: (0, nf + f)),
                  _layer_spec(layer, (tf, K), lambda i, f: (f, 0))],
        out_specs=pl.BlockSpec((tm, K), lambda i, f: (i, 0)),
        out_shape=jax.ShapeDtypeStruct((S, K), F32),
        scratch_shapes=[pltpu.VMEM((tm + FFN_HALO, K), BF16)],
        compiler_params=_cparams(("parallel", "arbitrary"), 56),
    )(x, x, gains, w_up16, w_up16, conv_w, conv_w, conv_b, conv_b, w_down16)


def _final_norm_kernel(x_ref, g_ref, o_ref):
    o_ref[...] = _rms(x_ref[...], g_ref[...])


def _final_norm(x, gain, tm):
    S, K = x.shape
    return pl.pallas_call(
        _final_norm_kernel,
        grid=(S // tm,),
        in_specs=[pl.BlockSpec((tm, K), lambda i: (i, 0)), pl.BlockSpec((1, K), lambda i: (0, 0))],
        out_specs=pl.BlockSpec((tm, K), lambda i: (i, 0)),
        out_shape=jax.ShapeDtypeStruct((S, K), F32),
        compiler_params=_cparams(("parallel",), 40),
    )(x, gain.reshape(1, K))


def kernel(x, mem, positions, mix_norm, w_in, ret_norm, hgrn_lb_logits, hgrn_norm, nsa_pos_k, nsa_pos_v,
           nsa_w_ck, nsa_w_cv, w_out, xattn_norm, mem_norm, xattn_wq, xattn_wk, xattn_wv, xattn_wo,
           ffn_norm, ffn_w_up, ffn_conv_w, ffn_conv_b, ffn_w_down, final_norm):
    B, S, _ = x.shape
    assert B == 1 and S % 1024 == 0
    L = DEPTH
    xs = x.reshape(S, D_MODEL)
    mem2 = mem.reshape(N_MEM, D_MODEL)
    row = lambda g: g.reshape(L, 1, g.shape[-1])
    w_in16 = _regroup_w_in(w_in)
    w_out16 = w_out.astype(BF16)
    wq16, wk16, wv16, wo16 = (w.astype(BF16) for w in (xattn_wq, xattn_wk, xattn_wv, xattn_wo))
    w_up16 = ffn_w_up.astype(BF16)
    w_down16 = ffn_w_down.astype(BF16)
    wck16 = nsa_w_ck.reshape(L, NSA_CMP_LEN * HEAD_DIM, HEAD_DIM).astype(BF16)
    wcv16 = nsa_w_cv.reshape(L, NSA_CMP_LEN * HEAD_DIM, HEAD_DIM).astype(BF16)
    pos_k = nsa_pos_k.reshape(L, 1, NSA_CMP_LEN * HEAD_DIM)
    pos_v = nsa_pos_v.reshape(L, 1, NSA_CMP_LEN * HEAD_DIM)
    mix_g, ret_g, hgrn_g, xattn_g, mem_g, ffn_g = (
        row(g) for g in (mix_norm, ret_norm, hgrn_norm, xattn_norm, mem_norm, ffn_norm))
    conv_b = row(ffn_conv_b)

    cos, sin = _rope_tables(positions, S)
    for layer in range(L):
        p32, p16 = _inproj(xs, mix_g, w_in16, layer, PROJ_TM, PROJ_TN)
        o_ret = _retention(p32, p16, cos, sin, ret_g, layer, S)
        o_sb = _stick_breaking(p16, S)
        o_hg = _hgrn2(p32, p16, hgrn_lb_logits, hgrn_g, layer, S)
        qn16, kc16, vc16, ks16, kw16, vs16, vw16 = _nsa_prep(p32, p16, cos, sin, S)
        kcmp, vcmp = _nsa_compress(kc16, vc16, wck16, wcv16, pos_k, pos_v, layer, S)
        o_nsa = _nsa_attention(qn16, p32, kcmp, vcmp, ks16, vs16, kw16, vw16, S)
        xs = _outproj((o_ret, o_sb, o_hg, o_nsa), w_out16, layer, xs, ROW_TM)
        k16 = _normproj(mem2, mem_g, wk16, layer, BF16)
        v16 = _normproj(mem2, mem_g, wv16, layer, BF16)
        xs = _xattn(xs, xattn_g, wq16, k16, v16, wo16, layer, ROW_TM)
        xs = _ffn(xs, ffn_g, w_up16, ffn_conv_w, conv_b, w_down16, layer, FFN_TM, FFN_TF)
    return _final_norm(xs, final_norm, ROW_TM).reshape(B, S, D_MODEL)
```

```python
import functools
import math

import numpy as np
import jax
import jax.numpy as jnp
from jax import lax
from jax.experimental import pallas as pl
from jax.experimental.pallas import tpu as pltpu

F32 = jnp.float32
BF16 = jnp.bfloat16

D_MODEL = 2048
DEPTH = 4
N_HEADS = 4
HEAD_DIM = 128
GROUP_WIDTH = N_HEADS * HEAD_DIM
ROPE_THETA = 10000.0
RET_CHUNK = 128
HGRN_CHUNK = 64
NSA_CMP_LEN = 32
NSA_CMP_STRIDE = 16
NSA_SLC_LEN = 64
NSA_SLC_SHIFT = 6
NSA_TOPK = 16
NSA_WINDOW = 512
N_MEM = 256
D_FF = 5632
NORM_EPS = 1e-6
MASK_VALUE = -1e30
FORCED_SCORE = 1e4
MIN_FORGET = 1e-6
IN_COLS = 12 * GROUP_WIDTH + 6 * HEAD_DIM + 3 * N_HEADS
ATTN_SCALE = HEAD_DIM ** -0.5
LANES = 128


def _in_cols_layout():
    G, D = GROUP_WIDTH, HEAD_DIM
    names = ["rq", "rk", "rv", "rg", "sq", "sk", "sv", "gq", "gf", "gi", "gg", "nq",
             "kc", "vc", "ks", "vs", "kw", "vw", "ng"]
    widths = [G] * 12 + [D] * 6 + [3 * N_HEADS]
    start = dict(zip(names, np.cumsum([0] + widths[:-1])))
    width = dict(zip(names, widths))
    f32_names = ["rq", "rk", "rg", "gq", "gf", "gg", "nq", "kc", "ks", "kw", "ng"]
    bf16_names = ["rv", "sq", "sk", "sv", "gi", "vc", "vs", "vw"]
    col, srcs = {}, []
    for group in (f32_names, bf16_names):
        pos, src = 0, []
        for n in group:
            assert pos % LANES == 0
            col[n] = pos // LANES
            src.append((int(start[n]), width[n]))
            pos += width[n]
        srcs.append(src)
    return srcs[0], srcs[1], col


_P32_SRC, _P16_SRC, _COL = _in_cols_layout()
P32_COLS = 4096
P16_COLS = 3072
COL_RQ, COL_RK, COL_RG = _COL["rq"], _COL["rk"], _COL["rg"]
COL_GQ, COL_GF, COL_GG, COL_NQ = _COL["gq"], _COL["gf"], _COL["gg"], _COL["nq"]
COL_KC, COL_KS, COL_KW, COL_NG = _COL["kc"], _COL["ks"], _COL["kw"], _COL["ng"]
COL_RV, COL_SQ, COL_SK, COL_SV, COL_GI = _COL["rv"], _COL["sq"], _COL["sk"], _COL["sv"], _COL["gi"]
COL_VC, COL_VS, COL_VW = _COL["vc"], _COL["vs"], _COL["vw"]

V7X_VMEM_BYTES = 64 * 1024 * 1024

PROJ_TM, PROJ_TN = 1024, 1024
ROW_TM = 512
FFN_TM, FFN_TF = 1024, 512


def _cparams(semantics, vmem_mb):
    return pltpu.CompilerParams(dimension_semantics=semantics,
                                vmem_limit_bytes=min(vmem_mb * 1024 * 1024, V7X_VMEM_BYTES - (8 << 20)))


def _layer_spec(layer, tail_shape, tail_index=None):
    if tail_index is None:
        tail_index = lambda *ids: (0,) * len(tail_shape)
    return pl.BlockSpec((None,) + tuple(tail_shape), lambda *ids: (layer,) + tuple(tail_index(*ids)))


def _dot(a, b):
    return jnp.dot(a, b, preferred_element_type=F32)


def _dot_nt(a, b):
    return lax.dot_general(a, b, (((1,), (1,)), ((), ())), preferred_element_type=F32)


def _dot_tn(a, b):
    return lax.dot_general(a, b, (((0,), (0,)), ((), ())), preferred_element_type=F32)


def _split3(x):
    h1 = x.astype(BF16)
    r1 = x - h1.astype(F32)
    h2 = r1.astype(BF16)
    r2 = r1 - h2.astype(F32)
    return h1, h2, r2.astype(BF16)


def _rms(x, gain):
    ms = jnp.mean(x * x, axis=-1, keepdims=True)
    return x * lax.rsqrt(ms + NORM_EPS) * gain


def _rope(x, cos, sin_signed):
    return x * cos + pltpu.roll(x, HEAD_DIM // 2, axis=1) * sin_signed


def _silu(x):
    return x * jax.nn.sigmoid(x)


def _rope_table_kernel(pos_ref, inv_ref, cos_ref, sin_ref):
    ang = pos_ref[...].astype(F32) * inv_ref[...]
    lane = lax.broadcasted_iota(jnp.int32, ang.shape, 1)
    cos_ref[...] = jnp.cos(ang)
    s = jnp.sin(ang)
    sin_ref[...] = jnp.where(lane < HEAD_DIM // 2, -s, s)


def _rope_tables(positions, S):
    half = HEAD_DIM // 2
    inv = ROPE_THETA ** (-jnp.arange(half, dtype=F32) / half)
    inv = jnp.concatenate([inv, inv]).reshape(1, HEAD_DIM)
    pos = positions.reshape(S, 1)
    tm = min(S, 1024)
    return pl.pallas_call(
        _rope_table_kernel,
        grid=(S // tm,),
        in_specs=[pl.BlockSpec((tm, 1), lambda i: (i, 0)),
                  pl.BlockSpec((1, HEAD_DIM), lambda i: (0, 0))],
        out_specs=[pl.BlockSpec((tm, HEAD_DIM), lambda i: (i, 0)),
                   pl.BlockSpec((tm, HEAD_DIM), lambda i: (i, 0))],
        out_shape=[jax.ShapeDtypeStruct((S, HEAD_DIM), F32)] * 2,
        compiler_params=_cparams(("arbitrary",), 32),
    )(pos, inv)


def _inproj_kernel(n32, x_ref, g_ref, w_ref, o32_ref, o16_ref, h_ref):
    j = pl.program_id(1)

    @pl.when(j == 0)
    def _():
        h_ref[...] = _rms(x_ref[...], g_ref[...]).astype(BF16)

    @pl.when(j < n32)
    def _():
        o32_ref[...] = _dot(h_ref[...], w_ref[...])

    @pl.when(j >= n32)
    def _():
        o16_ref[...] = _dot(h_ref[...], w_ref[...]).astype(BF16)


def _inproj(x, gains, w16, layer, tm, tn):
    S, K = x.shape
    n32, n16 = P32_COLS // tn, P16_COLS // tn
    assert w16.shape[2] == (n32 + n16) * tn
    return pl.pallas_call(
        functools.partial(_inproj_kernel, n32),
        grid=(S // tm, n32 + n16),
        in_specs=[pl.BlockSpec((tm, K), lambda i, j: (i, 0)),
                  _layer_spec(layer, (1, K)),
                  _layer_spec(layer, (K, tn), lambda i, j: (0, j))],
        out_specs=[pl.BlockSpec((tm, tn), lambda i, j: (i, jnp.minimum(j, n32 - 1))),
                   pl.BlockSpec((tm, tn), lambda i, j: (i, jnp.maximum(j - n32, 0)))],
        out_shape=[jax.ShapeDtypeStruct((S, P32_COLS), F32), jax.ShapeDtypeStruct((S, P16_COLS), BF16)],
        scratch_shapes=[pltpu.VMEM((tm, K), BF16)],
        compiler_params=_cparams(("parallel", "arbitrary"), 48),
    )(x, gains, w16)


def _regroup_plan():
    plan = []
    for src, width in ((_P32_SRC, P32_COLS), (_P16_SRC, P16_COLS)):
        narrow = []
        for a, n in src:
            if n == GROUP_WIDTH:
                plan.append(("wide", a // GROUP_WIDTH))
            else:
                narrow.append((a // LANES, n))
        assert len(narrow) <= GROUP_WIDTH // LANES
        plan.append(("narrow", narrow))
    assert len(plan) * GROUP_WIDTH == P32_COLS + P16_COLS
    return plan


def _regroup_kernel(plan, tbl_ref, wide_ref, *refs):
    narrow_refs, o_ref = refs[:-1], refs[-1]
    b = pl.program_id(2)
    rows = o_ref.shape[0]
    narrow_tiles = [i for i, p in enumerate(plan) if p[0] == "narrow"]
    is_narrow = functools.reduce(jnp.logical_or, [b == i for i in narrow_tiles])

    @pl.when(jnp.logical_not(is_narrow))
    def _():
        o_ref[...] = wide_ref[...].astype(BF16)

    used = 0
    for i in narrow_tiles:
        pieces = plan[i][1]

        @pl.when(b == i)
        def _(pieces=pieces, used=used):
            lane = lax.broadcasted_iota(jnp.int32, (rows, LANES), 1)
            for j in range(GROUP_WIDTH // LANES):
                if j < len(pieces):
                    blk = jnp.where(lane < pieces[j][1], narrow_refs[used + j][...], 0.0)
                else:
                    blk = jnp.zeros((rows, LANES), F32)
                o_ref[:, j * LANES:(j + 1) * LANES] = blk.astype(BF16)

        used += len(pieces)


def _regroup_w_in(w_in):
    L, K, _ = w_in.shape
    plan = _regroup_plan()
    rows = K // 2
    table = jnp.asarray([p[1] if p[0] == "wide" else 0 for p in plan], jnp.int32)
    narrow_blocks = [blk for p in plan if p[0] == "narrow" for blk, _ in p[1]]
    grid_spec = pltpu.PrefetchScalarGridSpec(
        num_scalar_prefetch=1,
        grid=(L, K // rows, len(plan)),
        in_specs=[pl.BlockSpec((None, rows, GROUP_WIDTH), lambda l, r, b, tbl: (l, r, tbl[b]))]
        + [pl.BlockSpec((None, rows, LANES), functools.partial(lambda blk, l, r, b, tbl: (l, r, blk), blk))
           for blk in narrow_blocks],
        out_specs=pl.BlockSpec((None, rows, GROUP_WIDTH), lambda l, r, b, tbl: (l, r, b)),
    )
    return pl.pallas_call(
        functools.partial(_regroup_kernel, plan),
        grid_spec=grid_spec,
        out_shape=jax.ShapeDtypeStruct((L, K, P32_COLS + P16_COLS), BF16),
        compiler_params=_cparams(("parallel", "parallel", "arbitrary"), 32),
    )(table, w_in, *([w_in] * len(narrow_blocks)))


def _normproj_kernel(x_ref, g_ref, w_ref, o_ref):
    h = _rms(x_ref[...], g_ref[...]).astype(BF16)
    o_ref[...] = _dot(h, w_ref[...]).astype(o_ref.dtype)


def _normproj(x, gains, w16, layer, out_dtype):
    M, K = x.shape
    N = w16.shape[2]
    return pl.pallas_call(
        _normproj_kernel,
        grid=(1,),
        in_specs=[pl.BlockSpec((M, K), lambda i: (0, 0)),
                  _layer_spec(layer, (1, K)),
                  _layer_spec(layer, (K, N))],
        out_specs=pl.BlockSpec((M, N), lambda i: (0, 0)),
        out_shape=jax.ShapeDtypeStruct((M, N), out_dtype),
        compiler_params=_cparams(("arbitrary",), 32),
    )(x, gains, w16)


_RET_LOG_GAMMA = [math.log1p(-(2.0 ** (-5.0 - h))) for h in range(N_HEADS)]
RET_STEP = 4


def _ret_kernel(q_ref, k_ref, v_ref, g_ref, cos_ref, sin_ref, gain_ref, o_ref, st_ref):
    C = RET_CHUNK

    @pl.when(pl.program_id(0) == 0)
    def _():
        st_ref[...] = jnp.zeros_like(st_ref)

    row = lax.broadcasted_iota(jnp.int32, (C, C), 0)
    col = lax.broadcasted_iota(jnp.int32, (C, C), 1)
    rel = (row - col).astype(F32)
    rowf = row.astype(F32)
    for h in range(N_HEADS):
        hs = slice(h * HEAD_DIM, (h + 1) * HEAD_DIM)
        lg = _RET_LOG_GAMMA[h]
        decay = jnp.where(rel >= 0, jnp.exp(lg * jnp.maximum(rel, 0.0)), 0.0)
        q_w = jnp.exp(lg * (rowf + 1.0))
        k_w = jnp.exp(lg * (C - 1.0 - rowf))
        st = st_ref[h]
        for c in range(RET_STEP):
            rs = slice(c * C, (c + 1) * C)
            cos = cos_ref[rs, :]
            sin = sin_ref[rs, :]
            q = _rope(q_ref[rs, hs], cos, sin)
            k = _rope(k_ref[rs, hs], cos, sin) * ATTN_SCALE
            v = v_ref[rs, hs]
            s = _dot_nt(q.astype(BF16), k.astype(BF16)) * decay
            o = _dot(s.astype(BF16), v) + _dot_nt((q * q_w).astype(BF16), st.astype(BF16))
            st = st * math.exp(lg * C) + _dot_tn(v, (k * k_w).astype(BF16))
            oc = o - jnp.mean(o, axis=-1, keepdims=True)
            y = oc * lax.rsqrt(jnp.mean(oc * oc, axis=-1, keepdims=True) + NORM_EPS) * gain_ref[:, hs]
            o_ref[rs, hs] = (y * _silu(g_ref[rs, hs])).astype(o_ref.dtype)
        st_ref[h] = st


def _retention(p32, p16, cos, sin, gains, layer, S):
    rows = RET_CHUNK * RET_STEP
    gw = GROUP_WIDTH // LANES
    return pl.pallas_call(
        _ret_kernel,
        grid=(S // rows,),
        in_specs=[pl.BlockSpec((rows, GROUP_WIDTH), lambda c: (c, COL_RQ // gw)),
                  pl.BlockSpec((rows, GROUP_WIDTH), lambda c: (c, COL_RK // gw)),
                  pl.BlockSpec((rows, GROUP_WIDTH), lambda c: (c, COL_RV // gw)),
                  pl.BlockSpec((rows, GROUP_WIDTH), lambda c: (c, COL_RG // gw)),
                  pl.BlockSpec((rows, HEAD_DIM), lambda c: (c, 0)),
                  pl.BlockSpec((rows, HEAD_DIM), lambda c: (c, 0)),
                  _layer_spec(layer, (1, GROUP_WIDTH))],
        out_specs=pl.BlockSpec((rows, GROUP_WIDTH), lambda c: (c, 0)),
        out_shape=jax.ShapeDtypeStruct((S, GROUP_WIDTH), BF16),
        scratch_shapes=[pltpu.VMEM((N_HEADS, HEAD_DIM, HEAD_DIM), F32)],
        compiler_params=_cparams(("arbitrary",), 32),
    )(p32, p32, p16, p32, cos, sin, gains)


SB_TILE = 256
SB_HEADS = 2
SB_DEAD_LOG = -104.0


def _sb_tile(q, k_ref, v_ref, hs, j, run, acc, upper, diag, on=None):
    T = SB_TILE
    start = pl.multiple_of(j * T, T)
    k = k_ref[pl.ds(start, T), hs]
    v = v_ref[pl.ds(start, T), hs]
    z = _dot_nt(q, k) * ATTN_SCALE
    sp = jnp.maximum(z, 0.0) + jnp.log(1.0 + jnp.exp(-jnp.abs(z)))
    log_1m = -sp
    strict = on
    if diag:
        row = lax.broadcasted_iota(jnp.int32, (T, T), 0)
        col = lax.broadcasted_iota(jnp.int32, (T, T), 1)
        strict = col < row
    if strict is not None:
        log_1m = jnp.where(strict, log_1m, 0.0)
    hi = log_1m.astype(BF16)
    lo = (log_1m - hi.astype(F32)).astype(BF16)
    both = _dot(jnp.concatenate([hi, lo], axis=0), upper)
    between = both[:T] + both[T:] + run
    w = jnp.exp((z - sp) + between)
    if strict is not None:
        w = jnp.where(strict, w, 0.0)
    acc = acc + _dot(w.astype(BF16), v)
    run = between[:, 0:1] + log_1m[:, 0:1]
    return run, acc


def _sb_kernel(q_ref, k_ref, v_ref, o_ref):
    T = SB_TILE
    i = pl.program_id(1)
    r = lax.broadcasted_iota(jnp.int32, (T, T), 0)
    c = lax.broadcasted_iota(jnp.int32, (T, T), 1)
    upper = jnp.where(r > c, 1.0, 0.0).astype(BF16)

    def alive(run):
        return (jnp.max(run) > SB_DEAD_LOG).astype(jnp.int32)

    heads = []
    for h in range(SB_HEADS):
        hs = slice(h * HEAD_DIM, (h + 1) * HEAD_DIM)
        q = q_ref[:, hs]
        run = jnp.zeros((T, 1), F32)
        acc = jnp.zeros((T, HEAD_DIM), F32)
        run, acc = _sb_tile(q, k_ref, v_ref, hs, i, run, acc, upper, True)
        run, acc = _sb_tile(q, k_ref, v_ref, hs, jnp.maximum(i - 1, 0), run, acc, upper, False, on=i > 0)
        heads.append((hs, q, run, acc))

    for hs, q, run, acc in heads:
        def cond(carry):
            step, live, _, _ = carry
            return jnp.logical_and(step < i, live > 0)

        def body(carry, hs=hs, q=q):
            step, _, run, acc = carry
            run, acc = _sb_tile(q, k_ref, v_ref, hs, i - 1 - step, run, acc, upper, False)
            return step + 1, alive(run), run, acc

        _, _, _, acc = lax.while_loop(cond, body, (jnp.int32(1), alive(run), run, acc))
        o_ref[:, hs] = acc.astype(o_ref.dtype)


def _stick_breaking(p16, S):
    T = SB_TILE
    W = SB_HEADS * HEAD_DIM
    return pl.pallas_call(
        _sb_kernel,
        grid=(N_HEADS // SB_HEADS, S // T),
        in_specs=[pl.BlockSpec((T, W), lambda h, i: (i, COL_SQ // SB_HEADS + h)),
                  pl.BlockSpec((S, W), lambda h, i: (0, COL_SK // SB_HEADS + h)),
                  pl.BlockSpec((S, W), lambda h, i: (0, COL_SV // SB_HEADS + h))],
        out_specs=pl.BlockSpec((T, W), lambda h, i: (i, h)),
        out_shape=jax.ShapeDtypeStruct((S, GROUP_WIDTH), BF16),
        compiler_params=_cparams(("parallel", "arbitrary"), 40),
    )(p16, p16, p16)


_HG_LEVELS = (32, 16, 8, 4, 2, 1)
HGRN_STEP = 2


def _hgrn_constants():
    C = HGRN_CHUNK
    n = np.arange(C)
    seg = np.zeros((8 * C, C), np.float32)
    pair = np.zeros((7, C, C), np.float32)
    for l, s in enumerate(_HG_LEVELS):
        blk = n // s
        odd = (blk % 2) == 1
        a = blk * s
        e = a + s - 1
        j = n[None, :]
        t_odd = (j >= a[:, None]) & (j <= n[:, None])
        t_even = (j > n[:, None]) & (j <= e[:, None])
        seg[l * C:(l + 1) * C] = np.where(odd[:, None], t_odd, t_even)
        pair[l] = odd[:, None] & (blk[None, :] == blk[:, None] - 1)
    seg[6 * C:7 * C] = n[None, :] <= n[:, None]
    seg[7 * C:8 * C] = n[None, :] > n[:, None]
    pair[6] = np.eye(C)
    return seg, pair


def _hgrn_kernel(layer, q_ref, f_ref, v_ref, g_ref, lbl_ref, gain_ref, seg_ref, pair_ref, o_ref, st_ref):
    C = HGRN_CHUNK

    @pl.when(pl.program_id(0) == 0)
    def _():
        st_ref[...] = jnp.zeros_like(st_ref)

    logits = lbl_ref[...]
    e = jnp.exp(logits - jnp.max(logits, axis=0, keepdims=True))
    p = e / jnp.sum(e, axis=0, keepdims=True)
    lb = jnp.zeros((1, GROUP_WIDTH), F32)
    for l in range(1, layer + 1):
        lb = lb + p[l:l + 1, :]

    f = lb + (1.0 - lb) * jax.nn.sigmoid(f_ref[...])
    log_f = jnp.log(jnp.maximum(f, MIN_FORGET))
    kk = 1.0 - f
    parts = _split3(log_f)
    zero = jnp.zeros((C, GROUP_WIDTH), BF16)
    rhs = jnp.concatenate(
        [jnp.concatenate([p[c * C:(c + 1) * C] for p in parts] + [zero], axis=0) for c in range(HGRN_STEP)],
        axis=1)
    ex = jnp.exp(_dot(seg_ref[...], rhs))

    for h in range(N_HEADS):
        hs = slice(h * HEAD_DIM, (h + 1) * HEAD_DIM)
        st = st_ref[h]
        for c in range(HGRN_STEP):
            rs = slice(c * C, (c + 1) * C)
            es = slice(c * GROUP_WIDTH + h * HEAD_DIM, c * GROUP_WIDTH + (h + 1) * HEAD_DIM)
            q = q_ref[rs, hs] * ATTN_SCALE
            k = kk[rs, hs]
            v = v_ref[rs, hs]
            s = pair_ref[6] * _dot_nt(q.astype(BF16), k.astype(BF16))
            for l in range(6):
                el = ex[l * C:(l + 1) * C, es]
                s = s + pair_ref[l] * _dot_nt((q * el).astype(BF16), (k * el).astype(BF16))
            e_cum = ex[6 * C:7 * C, es]
            e_rest = ex[7 * C:8 * C, es]
            o = _dot(s.astype(BF16), v) + _dot_nt((q * e_cum).astype(BF16), st.astype(BF16))
            st = st * e_cum[C - 1:C, :] + _dot_tn(v, (k * e_rest).astype(BF16))
            y = o * lax.rsqrt(jnp.mean(o * o, axis=-1, keepdims=True) + NORM_EPS) * gain_ref[:, hs]
            o_ref[rs, hs] = (y * _silu(g_ref[rs, hs])).astype(o_ref.dtype)
        st_ref[h] = st


def _hgrn2(p32, p16, lb_logits, gains, layer, S):
    C = HGRN_CHUNK
    rows = C * HGRN_STEP
    gw = GROUP_WIDTH // LANES
    seg, pair = _hgrn_constants()
    seg = np.concatenate([seg, seg, seg, np.zeros_like(seg)], axis=1)
    return pl.pallas_call(
        functools.partial(_hgrn_kernel, layer),
        grid=(S // rows,),
        in_specs=[pl.BlockSpec((rows, GROUP_WIDTH), lambda c: (c, COL_GQ // gw)),
                  pl.BlockSpec((rows, GROUP_WIDTH), lambda c: (c, COL_GF // gw)),
                  pl.BlockSpec((rows, GROUP_WIDTH), lambda c: (c, COL_GI // gw)),
                  pl.BlockSpec((rows, GROUP_WIDTH), lambda c: (c, COL_GG // gw)),
                  pl.BlockSpec((DEPTH, GROUP_WIDTH), lambda c: (0, 0)),
                  _layer_spec(layer, (1, GROUP_WIDTH)),
                  pl.BlockSpec((8 * C, 4 * C), lambda c: (0, 0)),
                  pl.BlockSpec((7, C, C), lambda c: (0, 0, 0))],
        out_specs=pl.BlockSpec((rows, GROUP_WIDTH), lambda c: (c, 0)),
        out_shape=jax.ShapeDtypeStruct((S, GROUP_WIDTH), BF16),
        scratch_shapes=[pltpu.VMEM((N_HEADS, HEAD_DIM, HEAD_DIM), F32)],
        compiler_params=_cparams(("arbitrary",), 32),
    )(p32, p32, p16, p32, lb_logits, gains,
      jnp.asarray(seg, BF16), jnp.asarray(pair, F32))


NSA_Q_SCALE = ATTN_SCALE * math.log2(math.e)
NSA_MAX_SLC = LANES


def _nsa_prep_kernel(q_ref, kc_ref, vc_ref, ks_ref, kw_ref, vs_ref, vw_ref, cos_ref, sin_ref,
                     qo_ref, kco_ref, vco_ref, kso_ref, kwo_ref, vso_ref, vwo_ref):
    tm = q_ref.shape[0]
    ones = jnp.ones((tm, HEAD_DIM), BF16)
    vso_ref[:, 0:HEAD_DIM] = vs_ref[...]
    vso_ref[:, HEAD_DIM:] = ones
    vwo_ref[:, 0:HEAD_DIM] = vw_ref[...]
    vwo_ref[:, HEAD_DIM:] = ones
    cos = cos_ref[...]
    sin = sin_ref[...]
    for h in range(N_HEADS):
        hs = slice(h * HEAD_DIM, (h + 1) * HEAD_DIM)
        qo_ref[:, hs] = (_rope(q_ref[:, hs], cos, sin) * NSA_Q_SCALE).astype(BF16)
    kco_ref[...] = _rope(kc_ref[...], cos, sin).astype(BF16)
    kwo_ref[...] = _rope(kw_ref[...], cos, sin).astype(BF16)
    vco_ref[...] = vc_ref[...]
    kso_ref[:, 0:HEAD_DIM] = _rope(ks_ref[...], cos, sin).astype(BF16)
    tok = pl.program_id(0) * tm + lax.broadcasted_iota(jnp.int32, (tm, NSA_MAX_SLC), 0)
    lane = lax.broadcasted_iota(jnp.int32, (tm, NSA_MAX_SLC), 1)
    kso_ref[:, HEAD_DIM:] = jnp.where(jnp.right_shift(tok, NSA_SLC_SHIFT) == lane, 1.0, 0.0).astype(BF16)


def _nsa_prep(p32, p16, cos, sin, S):
    tm = min(S, 512)
    gw = GROUP_WIDTH // LANES
    row = lambda col: pl.BlockSpec((tm, HEAD_DIM), lambda i: (i, col))
    out = pl.BlockSpec((tm, HEAD_DIM), lambda i: (i, 0))
    aug = pl.BlockSpec((tm, HEAD_DIM + NSA_MAX_SLC), lambda i: (i, 0))
    return pl.pallas_call(
        _nsa_prep_kernel,
        grid=(S // tm,),
        in_specs=[pl.BlockSpec((tm, GROUP_WIDTH), lambda i: (i, COL_NQ // gw)),
                  row(COL_KC), row(COL_VC), row(COL_KS), row(COL_KW), row(COL_VS), row(COL_VW), out, out],
        out_specs=[pl.BlockSpec((tm, GROUP_WIDTH), lambda i: (i, 0)), out, out, aug, out, aug, aug],
        out_shape=[jax.ShapeDtypeStruct((S, GROUP_WIDTH), BF16),
                   jax.ShapeDtypeStruct((S, HEAD_DIM), BF16),
                   jax.ShapeDtypeStruct((S, HEAD_DIM), BF16),
                   jax.ShapeDtypeStruct((S, HEAD_DIM + NSA_MAX_SLC), BF16),
                   jax.ShapeDtypeStruct((S, HEAD_DIM), BF16),
                   jax.ShapeDtypeStruct((S, 2 * HEAD_DIM), BF16),
                   jax.ShapeDtypeStruct((S, 2 * HEAD_DIM), BF16)],
        compiler_params=_cparams(("parallel",), 32),
    )(p32, p32, p16, p32, p32, p16, p16, cos, sin)


def _nsa_compress_kernel(kb_ref, vb_ref, wk_ref, wv_ref, pk_ref, pv_ref, ko_ref, vo_ref):
    nblk = kb_ref.shape[0]
    half = NSA_CMP_STRIDE * HEAD_DIM
    rowid = lax.broadcasted_iota(jnp.int32, (nblk, HEAD_DIM), 0)
    for b_ref, w_ref, p_ref, o_ref in ((kb_ref, wk_ref, pk_ref, ko_ref), (vb_ref, wv_ref, pv_ref, vo_ref)):
        w = w_ref[...]
        top = _dot(b_ref[...], w[:half])
        bot = _dot(b_ref[...], w[half:])
        ph, pm, plo = _split3(p_ref[...])
        const = _dot(ph, w) + _dot(pm, w) + _dot(plo, w)
        shifted = pltpu.roll(bot, nblk - 1, axis=0)
        out = top + shifted + const
        o_ref[...] = jnp.where(rowid < nblk - 1, out, 0.0).astype(BF16)


def _nsa_compress(kc16, vc16, wk16, wv16, pos_k, pos_v, layer, S):
    nblk = S // NSA_CMP_STRIDE
    width = NSA_CMP_STRIDE * HEAD_DIM
    kb = kc16.reshape(nblk, width)
    vb = vc16.reshape(nblk, width)
    full = lambda shape: pl.BlockSpec(shape, lambda i: (0,) * len(shape))
    return pl.pallas_call(
        _nsa_compress_kernel,
        grid=(1,),
        in_specs=[full((nblk, width)), full((nblk, width)),
                  _layer_spec(layer, (2 * width, HEAD_DIM)), _layer_spec(layer, (2 * width, HEAD_DIM)),
                  _layer_spec(layer, (1, 2 * width)), _layer_spec(layer, (1, 2 * width))],
        out_specs=[full((nblk, HEAD_DIM)), full((nblk, HEAD_DIM))],
        out_shape=[jax.ShapeDtypeStruct((nblk, HEAD_DIM), BF16)] * 2,
        compiler_params=_cparams(("arbitrary",), 32),
    )(kb, vb, wk16, wv16, pos_k, pos_v)


NSA_Q = 256
NSA_KT = 512
NSA_ROWS = N_HEADS * NSA_Q
NSA_OFF = -(2.0 ** 126)


def _lanes4(x):
    return jnp.maximum(jnp.maximum(x[:, 0:LANES], x[:, LANES:2 * LANES]),
                       jnp.maximum(x[:, 2 * LANES:3 * LANES], x[:, 3 * LANES:4 * LANES]))


def _nsa_attn_kernel(n_slc, q_ref, g_ref, kcmp_ref, vcmp_ref, ovt_ref, ks_ref, vs_ref, kw_ref, vw_ref,
                     o_ref, m_ref, acc_ref):
    Q, KT, R = NSA_Q, NSA_KT, NSA_ROWS
    nsl = NSA_MAX_SLC
    bi = pl.program_id(0)
    q0 = bi * Q
    ncmp = kcmp_ref.shape[0]
    qs = jnp.concatenate([q_ref[:, h * HEAD_DIM:(h + 1) * HEAD_DIM] for h in range(N_HEADS)], axis=0)
    qpos_r = q0 + jnp.bitwise_and(lax.broadcasted_iota(jnp.int32, (R, 1), 0), Q - 1)

    last_vis = jnp.minimum(jnp.right_shift(qpos_r - (NSA_CMP_LEN - 1), 4), ncmp - 2)
    cmp_mask = lax.broadcasted_iota(jnp.int32, (R, ncmp), 1) <= last_vis
    s2 = jnp.where(cmp_mask, _dot_nt(qs, kcmp_ref[...]), MASK_VALUE)
    pc = jnp.exp2(s2 - jnp.max(s2, axis=-1, keepdims=True))
    inv = jnp.where(last_vis >= 0, 1.0 / jnp.sum(pc, axis=-1, keepdims=True), 0.0)
    pc = pc * inv
    o_cmp = _dot(pc.astype(BF16), vcmp_ref[...])

    W = NSA_WINDOW + Q
    ws = pl.multiple_of(jnp.maximum(q0 - NSA_WINDOW, 0), Q)
    dist = (qpos_r - ws) - lax.broadcasted_iota(jnp.int32, (R, W), 1)
    wmask = pltpu.bitcast(dist, jnp.uint32) < NSA_WINDOW
    s2 = jnp.where(wmask, _dot_nt(qs, kw_ref[pl.ds(ws, W), :]), MASK_VALUE)
    pw = jnp.exp2(s2 - jnp.max(s2, axis=-1, keepdims=True))
    ow = _dot(pw.astype(BF16), vw_ref[pl.ds(ws, W), :])
    o_win = ow[:, 0:HEAD_DIM] / ow[:, HEAD_DIM:]

    p_sum = pc[0:Q] + pc[Q:2 * Q] + pc[2 * Q:3 * Q] + pc[3 * Q:4 * Q]
    ph = p_sum.astype(BF16)
    plo = (p_sum - ph.astype(F32)).astype(BF16)
    imp_t = _dot_nt(ovt_ref[...], ph) + _dot_nt(ovt_ref[...], plo)
    blk = lax.broadcasted_iota(jnp.int32, (nsl, Q), 0)
    blkf = blk.astype(F32)
    qpos_l = q0 + lax.broadcasted_iota(jnp.int32, (nsl, Q), 1)
    cur = jnp.right_shift(qpos_l, NSA_SLC_SHIFT)
    forced = (blk == 0) | (blk == cur) | (blk == cur - 1)
    future = blk * NSA_SLC_LEN > qpos_l
    work = jnp.where(future, -1.0, jnp.where(forced, FORCED_SCORE, imp_t))
    work = jnp.where(blk < n_slc, work, -jnp.inf)
    sel = jnp.zeros((nsl, Q), F32)
    for _ in range(min(NSA_TOPK, n_slc)):
        mx = jnp.max(work, axis=0, keepdims=True)
        first = jnp.min(jnp.where(work == mx, blkf, float(nsl)), axis=0, keepdims=True)
        hit = blkf == first
        sel = jnp.where(hit, 1.0, sel)
        work = jnp.where(hit, -jnp.inf, work)
    bias_t = jnp.where(future, NSA_OFF, jnp.where(sel > 0.0, 0.0, NSA_OFF))
    bias = bias_t.T.astype(BF16)
    q_aug = jnp.concatenate([qs, jnp.concatenate([bias] * N_HEADS, axis=0)], axis=1)

    n_tiles = (q0 + Q + KT - 1) // KT

    def scores(t, width, causal):
        start = pl.multiple_of(t * KT, KT)
        s2 = _dot_nt(q_aug, ks_ref[pl.ds(start, width), :])
        if causal:
            kpos = start + lax.broadcasted_iota(jnp.int32, (R, width), 1)
            s2 = jnp.where(kpos <= qpos_r, s2, NSA_OFF)
        return start, s2

    def max_tile(t, width, causal):
        _, s2 = scores(t, width, causal)
        m = m_ref[...]
        for c in range(width // KT):
            m = jnp.maximum(m, _lanes4(s2[:, c * KT:(c + 1) * KT]))
        m_ref[...] = m

    def sum_tile(t, width, causal):
        start, s2 = scores(t, width, causal)
        pt = jnp.exp2(s2 - jnp.concatenate([m_ref[...]] * (width // LANES), axis=1))
        acc_ref[...] += _dot(pt.astype(BF16), vs_ref[pl.ds(start, width), :])

    def loop(fn):
        n_full = n_tiles - 1

        def body(u, carry):
            fn(2 * u, 2 * KT, False)
            return carry

        lax.fori_loop(0, n_full // 2, body, 0)

        @pl.when(n_full % 2 == 1)
        def _():
            fn(n_full - 1, KT, False)

        fn(n_tiles - 1, KT, True)

    m_ref[...] = jnp.full_like(m_ref, NSA_OFF)
    loop(max_tile)
    m_ref[...] = jnp.broadcast_to(jnp.max(m_ref[...], axis=-1, keepdims=True), m_ref.shape)
    acc_ref[...] = jnp.zeros_like(acc_ref)
    loop(sum_tile)
    o_slc = acc_ref[:, 0:HEAD_DIM] / acc_ref[:, HEAD_DIM:]

    gate = jax.nn.sigmoid(g_ref[...])
    for h in range(N_HEADS):
        rows = slice(h * Q, (h + 1) * Q)
        out = (gate[:, 3 * h:3 * h + 1] * o_cmp[rows] + gate[:, 3 * h + 1:3 * h + 2] * o_slc[rows]
               + gate[:, 3 * h + 2:3 * h + 3] * o_win[rows])
        o_ref[:, h * HEAD_DIM:(h + 1) * HEAD_DIM] = out.astype(o_ref.dtype)


def _nsa_overlap(S):
    n_cmp = (S - NSA_CMP_LEN) // NSA_CMP_STRIDE + 1
    n_slc = S // NSA_SLC_LEN
    nblk = S // NSA_CMP_STRIDE
    nsl = NSA_MAX_SLC
    assert n_slc <= nsl
    cmp_start = np.arange(n_cmp) * NSA_CMP_STRIDE
    slc_start = np.arange(n_slc) * NSA_SLC_LEN
    ov = (np.minimum(cmp_start[:, None] + NSA_CMP_LEN, slc_start[None, :] + NSA_SLC_LEN)
          - np.maximum(cmp_start[:, None], slc_start[None, :]))
    out = np.zeros((nblk, nsl), np.float32)
    out[:n_cmp, :n_slc] = np.clip(ov, 0, None) / NSA_CMP_LEN
    return out, n_slc


def _nsa_attention(qn16, p32, kcmp, vcmp, ks16, vs16, kw16, vw16, S):
    Q = NSA_Q
    ov, n_slc = _nsa_overlap(S)
    nblk, nsl = ov.shape
    full = lambda shape: pl.BlockSpec(shape, lambda i: (0,) * len(shape))
    return pl.pallas_call(
        functools.partial(_nsa_attn_kernel, n_slc),
        grid=(S // Q,),
        in_specs=[pl.BlockSpec((Q, GROUP_WIDTH), lambda i: (i, 0)),
                  pl.BlockSpec((Q, HEAD_DIM), lambda i: (i, COL_NG)),
                  full((nblk, HEAD_DIM)), full((nblk, HEAD_DIM)), full((nsl, nblk)),
                  full((S, HEAD_DIM + NSA_MAX_SLC)), full((S, 2 * HEAD_DIM)),
                  full((S, HEAD_DIM)), full((S, 2 * HEAD_DIM))],
        out_specs=pl.BlockSpec((Q, GROUP_WIDTH), lambda i: (i, 0)),
        out_shape=jax.ShapeDtypeStruct((S, GROUP_WIDTH), BF16),
        scratch_shapes=[pltpu.VMEM((NSA_ROWS, LANES), F32), pltpu.VMEM((NSA_ROWS, 2 * HEAD_DIM), F32)],
        compiler_params=_cparams(("arbitrary",), 48),
    )(qn16, p32, kcmp, vcmp, jnp.asarray(ov.T, BF16), ks16, vs16, kw16, vw16)


def _outproj_kernel(a0_ref, a1_ref, a2_ref, a3_ref, w_ref, x_ref, o_ref):
    acc = x_ref[...]
    for g, a_ref in enumerate((a0_ref, a1_ref, a2_ref, a3_ref)):
        acc = acc + _dot(a_ref[...], w_ref[g * GROUP_WIDTH:(g + 1) * GROUP_WIDTH, :])
    o_ref[...] = acc


def _outproj(parts, w16, layer, x, tm):
    S, N = x.shape
    a_spec = pl.BlockSpec((tm, GROUP_WIDTH), lambda i: (i, 0))
    return pl.pallas_call(
        _outproj_kernel,
        grid=(S // tm,),
        in_specs=[a_spec, a_spec, a_spec, a_spec,
                  _layer_spec(layer, (4 * GROUP_WIDTH, N)),
                  pl.BlockSpec((tm, N), lambda i: (i, 0))],
        out_specs=pl.BlockSpec((tm, N), lambda i: (i, 0)),
        out_shape=jax.ShapeDtypeStruct((S, N), F32),
        compiler_params=_cparams(("parallel",), 48),
    )(*parts, w16, x)


def _xattn_kernel(x_ref, g_ref, wq_ref, k_ref, v_ref, wo_ref, o_ref):
    x = x_ref[...]
    h = _rms(x, g_ref[...]).astype(BF16)
    q = _dot(h, wq_ref[...]).astype(BF16)
    outs = []
    for hd in range(N_HEADS):
        hs = slice(hd * HEAD_DIM, (hd + 1) * HEAD_DIM)
        s = _dot_nt(q[:, hs], k_ref[:, hs]) * ATTN_SCALE
        m = jnp.max(s, axis=-1, keepdims=True)
        p = jnp.exp(s - m)
        p = p / jnp.sum(p, axis=-1, keepdims=True)
        outs.append(_dot(p.astype(BF16), v_ref[:, hs]).astype(BF16))
    o = jnp.concatenate(outs, axis=-1)
    o_ref[...] = x + _dot(o, wo_ref[...])


def _xattn(x, gains, wq16, k16, v16, wo16, layer, tm):
    S, K = x.shape
    XW = wq16.shape[2]
    M = k16.shape[0]
    return pl.pallas_call(
        _xattn_kernel,
        grid=(S // tm,),
        in_specs=[pl.BlockSpec((tm, K), lambda i: (i, 0)),
                  _layer_spec(layer, (1, K)),
                  _layer_spec(layer, (K, XW)),
                  pl.BlockSpec((M, XW), lambda i: (0, 0)),
                  pl.BlockSpec((M, XW), lambda i: (0, 0)),
                  _layer_spec(layer, (XW, K))],
        out_specs=pl.BlockSpec((tm, K), lambda i: (i, 0)),
        out_shape=jax.ShapeDtypeStruct((S, K), F32),
        compiler_params=_cparams(("parallel",), 48),
    )(x, gains, wq16, k16, v16, wo16)


FFN_HALO = 16
FFN_SUB = 256


def _ffn_kernel(final, x_ref, xp_ref, g_ref, wg_ref, wv_ref, cwg_ref, cwv_ref, cbg_ref, cbv_ref, wd_ref,
                fg_ref, o_ref, h_ref):
    i = pl.program_id(0)
    f = pl.program_id(1)
    tf = wg_ref.shape[1]

    @pl.when(f == 0)
    def _():
        hp = _rms(xp_ref[...], g_ref[...])
        h_ref[0:FFN_HALO, :] = jnp.where(i > 0, hp, 0.0).astype(BF16)
        h_ref[FFN_HALO:, :] = _rms(x_ref[...], g_ref[...]).astype(BF16)
        o_ref[...] = x_ref[...]

    h = h_ref[...]

    def conv(w_ref, cw_ref, cb_ref, cols):
        u = _dot(h, w_ref[:, cols])
        u1 = pltpu.roll(u, 1, axis=0)
        u2 = pltpu.roll(u, 2, axis=0)
        c = cb_ref[:, cols] + cw_ref[2:3, cols] * u + cw_ref[1:2, cols] * u1 + cw_ref[0:1, cols] * u2
        return c[FFN_HALO:, :]

    acts = []
    for s in range(tf // FFN_SUB):
        cols = slice(s * FFN_SUB, (s + 1) * FFN_SUB)
        gate = conv(wg_ref, cwg_ref, cbg_ref, cols)
        val = conv(wv_ref, cwv_ref, cbv_ref, cols)
        acts.append((_silu(gate) * val).astype(BF16))
    o_ref[...] += _dot(jnp.concatenate(acts, axis=1), wd_ref[...])

    if final:
        @pl.when(f == pl.num_programs(1) - 1)
        def _():
            o_ref[...] = _rms(o_ref[...], fg_ref[...])


def _ffn(x, gains, w_up16, conv_w, conv_b, w_down16, final_gain, layer, tm, tf):
    S, K = x.shape
    F = w_down16.shape[1]
    nf = F // tf
    hb = tm // FFN_HALO
    return pl.pallas_call(
        functools.partial(_ffn_kernel, layer == DEPTH - 1),
        grid=(S // tm, nf),
        in_specs=[pl.BlockSpec((tm, K), lambda i, f: (i, 0), pipeline_mode=pl.Buffered(1)),
                  pl.BlockSpec((FFN_HALO, K), lambda i, f: (jnp.maximum(i * hb - 1, 0), 0)),
                  _layer_spec(layer, (1, K)),
                  _layer_spec(layer, (K, tf), lambda i, f: (0, f)),
                  _layer_spec(layer, (K, tf), lambda i, f: (0, nf + f)),
                  _layer_spec(layer, (3, tf), lambda i, f: (0, f)),
                  _layer_spec(layer, (3, tf), lambda i, f: (0, nf + f)),
                  _layer_spec(layer, (1, tf), lambda i, f: (0, f)),
                  _layer_spec(layer, (1, tf), lambda i, f: (0, nf + f)),
                  _layer_spec(layer, (tf, K), lambda i, f: (f, 0)),
                  pl.BlockSpec((1, K), lambda i, f: (0, 0))],
        out_specs=pl.BlockSpec((tm, K), lambda i, f: (i, 0)),
        out_shape=jax.ShapeDtypeStruct((S, K), F32),
        scratch_shapes=[pltpu.VMEM((tm + FFN_HALO, K), BF16)],
        compiler_params=_cparams(("parallel", "arbitrary"), 56),
    )(x, x, gains, w_up16, w_up16, conv_w, conv_w, conv_b, conv_b, w_down16, final_gain)


def kernel(x, mem, positions, mix_norm, w_in, ret_norm, hgrn_lb_logits, hgrn_norm, nsa_pos_k, nsa_pos_v,
           nsa_w_ck, nsa_w_cv, w_out, xattn_norm, mem_norm, xattn_wq, xattn_wk, xattn_wv, xattn_wo,
           ffn_norm, ffn_w_up, ffn_conv_w, ffn_conv_b, ffn_w_down, final_norm):
    B, S, _ = x.shape
    assert B == 1 and S % 1024 == 0
    L = DEPTH
    xs = x.reshape(S, D_MODEL)
    mem2 = mem.reshape(N_MEM, D_MODEL)
    row = lambda g: g.reshape(L, 1, g.shape[-1])
    w_in16 = _regroup_w_in(w_in)
    w_out16 = w_out.astype(BF16)
    wq16, wk16, wv16, wo16 = (w.astype(BF16) for w in (xattn_wq, xattn_wk, xattn_wv, xattn_wo))
    w_up16 = ffn_w_up.astype(BF16)
    w_down16 = ffn_w_down.astype(BF16)
    wck16 = nsa_w_ck.reshape(L, NSA_CMP_LEN * HEAD_DIM, HEAD_DIM).astype(BF16)
    wcv16 = nsa_w_cv.reshape(L, NSA_CMP_LEN * HEAD_DIM, HEAD_DIM).astype(BF16)
    pos_k = nsa_pos_k.reshape(L, 1, NSA_CMP_LEN * HEAD_DIM)
    pos_v = nsa_pos_v.reshape(L, 1, NSA_CMP_LEN * HEAD_DIM)
    mix_g, ret_g, hgrn_g, xattn_g, mem_g, ffn_g = (
        row(g) for g in (mix_norm, ret_norm, hgrn_norm, xattn_norm, mem_norm, ffn_norm))
    conv_b = row(ffn_conv_b)
    final_g = final_norm.reshape(1, D_MODEL)

    cos, sin = _rope_tables(positions, S)
    for layer in range(L):
        p32, p16 = _inproj(xs, mix_g, w_in16, layer, PROJ_TM, PROJ_TN)
        o_ret = _retention(p32, p16, cos, sin, ret_g, layer, S)
        o_sb = _stick_breaking(p16, S)
        o_hg = _hgrn2(p32, p16, hgrn_lb_logits, hgrn_g, layer, S)
        qn16, kc16, vc16, ks16, kw16, vs16, vw16 = _nsa_prep(p32, p16, cos, sin, S)
        kcmp, vcmp = _nsa_compress(kc16, vc16, wck16, wcv16, pos_k, pos_v, layer, S)
        o_nsa = _nsa_attention(qn16, p32, kcmp, vcmp, ks16, vs16, kw16, vw16, S)
        xs = _outproj((o_ret, o_sb, o_hg, o_nsa), w_out16, layer, xs, ROW_TM)
        k16 = _normproj(mem2, mem_g, wk16, layer, BF16)
        v16 = _normproj(mem2, mem_g, wv16, layer, BF16)
        xs = _xattn(xs, xattn_g, wq16, k16, v16, wo16, layer, ROW_TM)
        xs = _ffn(xs, ffn_g, w_up16, ffn_conv_w, conv_b, w_down16, final_g, layer, FFN_TM, FFN_TF)
    return xs.reshape(B, S, D_MODEL)
```

```python
import functools
import math

import numpy as np
import jax
import jax.numpy as jnp
from jax import lax
from jax.experimental import pallas as pl
from jax.experimental.pallas import tpu as pltpu

F32 = jnp.float32
BF16 = jnp.bfloat16

D_MODEL = 2048
DEPTH = 4
N_HEADS = 4
HEAD_DIM = 128
GROUP_WIDTH = N_HEADS * HEAD_DIM
ROPE_THETA = 10000.0
RET_CHUNK = 128
HGRN_CHUNK = 64
NSA_CMP_LEN = 32
NSA_CMP_STRIDE = 16
NSA_SLC_LEN = 64
NSA_SLC_SHIFT = 6
NSA_TOPK = 16
NSA_WINDOW = 512
N_MEM = 256
D_FF = 5632
NORM_EPS = 1e-6
MASK_VALUE = -1e30
FORCED_SCORE = 1e4
MIN_FORGET = 1e-6
IN_COLS = 12 * GROUP_WIDTH + 6 * HEAD_DIM + 3 * N_HEADS
ATTN_SCALE = HEAD_DIM ** -0.5
LANES = 128


IN_COLS_PAD = 7168

COL_RQ, COL_RK, COL_RV, COL_RG = 0, 4, 8, 12
COL_SQ, COL_SK, COL_SV = 16, 20, 24
COL_GQ, COL_GF, COL_GI, COL_GG = 28, 32, 36, 40
COL_NQ = 44
COL_KC, COL_VC, COL_KS, COL_VS, COL_KW, COL_VW, COL_NG = 48, 49, 50, 51, 52, 53, 54
_F32_PIECES = ((COL_RQ, 4), (COL_RK, 4), (COL_RG, 4), (COL_GQ, 4), (COL_GF, 4), (COL_GG, 4), (COL_NQ, 4),
               (COL_KC, 1), (COL_KS, 1), (COL_KW, 1), (COL_NG, 1))
_BF16_PIECES = ((COL_RV, 4), (COL_SQ, 4), (COL_SK, 4), (COL_SV, 4), (COL_GI, 4),
                (COL_VC, 1), (COL_VS, 1), (COL_VW, 1))

V7X_VMEM_BYTES = 64 * 1024 * 1024

PROJ_TM, PROJ_TN = 1024, 1024
ROW_TM = 512
FFN_TM, FFN_TF = 1024, 512


def _cparams(semantics, vmem_mb):
    return pltpu.CompilerParams(dimension_semantics=semantics,
                                vmem_limit_bytes=min(vmem_mb * 1024 * 1024, V7X_VMEM_BYTES - (8 << 20)))


def _layer_spec(layer, tail_shape, tail_index=None):
    if tail_index is None:
        tail_index = lambda *ids: (0,) * len(tail_shape)
    return pl.BlockSpec((None,) + tuple(tail_shape), lambda *ids: (layer,) + tuple(tail_index(*ids)))


def _dot(a, b):
    return jnp.dot(a, b, preferred_element_type=F32)


def _dot_nt(a, b):
    return lax.dot_general(a, b, (((1,), (1,)), ((), ())), preferred_element_type=F32)


def _dot_tn(a, b):
    return lax.dot_general(a, b, (((0,), (0,)), ((), ())), preferred_element_type=F32)


def _split3(x):
    h1 = x.astype(BF16)
    r1 = x - h1.astype(F32)
    h2 = r1.astype(BF16)
    r2 = r1 - h2.astype(F32)
    return h1, h2, r2.astype(BF16)


def _rms(x, gain):
    ms = jnp.mean(x * x, axis=-1, keepdims=True)
    return x * lax.rsqrt(ms + NORM_EPS) * gain


def _rope(x, cos, sin_signed):
    return x * cos + pltpu.roll(x, HEAD_DIM // 2, axis=1) * sin_signed


def _silu(x):
    return x * jax.nn.sigmoid(x)


def _rope_table_kernel(pos_ref, inv_ref, cos_ref, sin_ref):
    ang = pos_ref[...].astype(F32) * inv_ref[...]
    lane = lax.broadcasted_iota(jnp.int32, ang.shape, 1)
    cos_ref[...] = jnp.cos(ang)
    s = jnp.sin(ang)
    sin_ref[...] = jnp.where(lane < HEAD_DIM // 2, -s, s)


def _rope_tables(positions, S):
    half = HEAD_DIM // 2
    inv = ROPE_THETA ** (-jnp.arange(half, dtype=F32) / half)
    inv = jnp.concatenate([inv, inv]).reshape(1, HEAD_DIM)
    pos = positions.reshape(S, 1)
    tm = min(S, 1024)
    return pl.pallas_call(
        _rope_table_kernel,
        grid=(S // tm,),
        in_specs=[pl.BlockSpec((tm, 1), lambda i: (i, 0)),
                  pl.BlockSpec((1, HEAD_DIM), lambda i: (0, 0))],
        out_specs=[pl.BlockSpec((tm, HEAD_DIM), lambda i: (i, 0)),
                   pl.BlockSpec((tm, HEAD_DIM), lambda i: (i, 0))],
        out_shape=[jax.ShapeDtypeStruct((S, HEAD_DIM), F32)] * 2,
        compiler_params=_cparams(("arbitrary",), 32),
    )(pos, inv)


def _tile_needs(pieces, n_tiles, tn):
    blocks = tn // LANES
    return [any(j * blocks < start + n and start < (j + 1) * blocks for start, n in pieces)
            for j in range(n_tiles)]


def _resident_index(j, needed):
    written = [t for t, w in enumerate(needed) if w]
    idx = j
    for t, w in enumerate(needed):
        if not w:
            before = [u for u in written if u < t]
            idx = jnp.where(j == t, before[-1] if before else written[0], idx)
    return idx


def _inproj_kernel(need32, need16, x_ref, g_ref, w_ref, o32_ref, o16_ref, h_ref):
    j = pl.program_id(1)

    @pl.when(j == 0)
    def _():
        h_ref[...] = _rms(x_ref[...], g_ref[...]).astype(BF16)

    def is_one_of(tiles):
        return functools.reduce(jnp.logical_or, [j == t for t in tiles])

    both = [t for t in range(len(need32)) if need32[t] and need16[t]]
    only32 = [t for t in range(len(need32)) if need32[t] and not need16[t]]
    only16 = [t for t in range(len(need32)) if need16[t] and not need32[t]]

    if both:
        @pl.when(is_one_of(both))
        def _():
            y = _dot(h_ref[...], w_ref[...])
            o32_ref[...] = y
            o16_ref[...] = y.astype(BF16)

    if only32:
        @pl.when(is_one_of(only32))
        def _():
            o32_ref[...] = _dot(h_ref[...], w_ref[...])

    if only16:
        @pl.when(is_one_of(only16))
        def _():
            o16_ref[...] = _dot(h_ref[...], w_ref[...]).astype(BF16)


def _inproj(x, gains, w16, layer, tm, tn):
    S, K = x.shape
    N = w16.shape[2]
    n_tiles = N // tn
    need32 = _tile_needs(_F32_PIECES, n_tiles, tn)
    need16 = _tile_needs(_BF16_PIECES, n_tiles, tn)
    return pl.pallas_call(
        functools.partial(_inproj_kernel, need32, need16),
        grid=(S // tm, n_tiles),
        in_specs=[pl.BlockSpec((tm, K), lambda i, j: (i, 0)),
                  _layer_spec(layer, (1, K)),
                  _layer_spec(layer, (K, tn), lambda i, j: (0, j))],
        out_specs=[pl.BlockSpec((tm, tn), lambda i, j: (i, _resident_index(j, need32))),
                   pl.BlockSpec((tm, tn), lambda i, j: (i, _resident_index(j, need16)))],
        out_shape=[jax.ShapeDtypeStruct((S, N), F32), jax.ShapeDtypeStruct((S, N), BF16)],
        scratch_shapes=[pltpu.VMEM((tm, K), BF16)],
        compiler_params=_cparams(("parallel", "arbitrary"), 48),
    )(x, gains, w16)


def _normproj_kernel(x_ref, g_ref, w_ref, o_ref):
    h = _rms(x_ref[...], g_ref[...]).astype(BF16)
    o_ref[...] = _dot(h, w_ref[...]).astype(o_ref.dtype)


def _normproj(x, gains, w16, layer, out_dtype):
    M, K = x.shape
    N = w16.shape[2]
    return pl.pallas_call(
        _normproj_kernel,
        grid=(1,),
        in_specs=[pl.BlockSpec((M, K), lambda i: (0, 0)),
                  _layer_spec(layer, (1, K)),
                  _layer_spec(layer, (K, N))],
        out_specs=pl.BlockSpec((M, N), lambda i: (0, 0)),
        out_shape=jax.ShapeDtypeStruct((M, N), out_dtype),
        compiler_params=_cparams(("arbitrary",), 32),
    )(x, gains, w16)


_RET_LOG_GAMMA = [math.log1p(-(2.0 ** (-5.0 - h))) for h in range(N_HEADS)]
RET_STEP = 4


def _ret_kernel(q_ref, k_ref, v_ref, g_ref, cos_ref, sin_ref, gain_ref, o_ref, st_ref):
    C = RET_CHUNK

    @pl.when(pl.program_id(0) == 0)
    def _():
        st_ref[...] = jnp.zeros_like(st_ref)

    row = lax.broadcasted_iota(jnp.int32, (C, C), 0)
    col = lax.broadcasted_iota(jnp.int32, (C, C), 1)
    rel = (row - col).astype(F32)
    rowf = row.astype(F32)
    for h in range(N_HEADS):
        hs = slice(h * HEAD_DIM, (h + 1) * HEAD_DIM)
        lg = _RET_LOG_GAMMA[h]
        decay = jnp.where(rel >= 0, jnp.exp(lg * jnp.maximum(rel, 0.0)), 0.0)
        q_w = jnp.exp(lg * (rowf + 1.0))
        k_w = jnp.exp(lg * (C - 1.0 - rowf))
        st = st_ref[h]
        for c in range(RET_STEP):
            rs = slice(c * C, (c + 1) * C)
            cos = cos_ref[rs, :]
            sin = sin_ref[rs, :]
            q = _rope(q_ref[rs, hs], cos, sin)
            k = _rope(k_ref[rs, hs], cos, sin) * ATTN_SCALE
            v = v_ref[rs, hs]
            s = _dot_nt(q.astype(BF16), k.astype(BF16)) * decay
            o = _dot(s.astype(BF16), v) + _dot_nt((q * q_w).astype(BF16), st.astype(BF16))
            st = st * math.exp(lg * C) + _dot_tn(v, (k * k_w).astype(BF16))
            oc = o - jnp.mean(o, axis=-1, keepdims=True)
            y = oc * lax.rsqrt(jnp.mean(oc * oc, axis=-1, keepdims=True) + NORM_EPS) * gain_ref[:, hs]
            o_ref[rs, hs] = (y * _silu(g_ref[rs, hs])).astype(o_ref.dtype)
        st_ref[h] = st


def _retention(p32, p16, cos, sin, gains, layer, S):
    rows = RET_CHUNK * RET_STEP
    gw = GROUP_WIDTH // LANES
    return pl.pallas_call(
        _ret_kernel,
        grid=(S // rows,),
        in_specs=[pl.BlockSpec((rows, GROUP_WIDTH), lambda c: (c, COL_RQ // gw)),
                  pl.BlockSpec((rows, GROUP_WIDTH), lambda c: (c, COL_RK // gw)),
                  pl.BlockSpec((rows, GROUP_WIDTH), lambda c: (c, COL_RV // gw)),
                  pl.BlockSpec((rows, GROUP_WIDTH), lambda c: (c, COL_RG // gw)),
                  pl.BlockSpec((rows, HEAD_DIM), lambda c: (c, 0)),
                  pl.BlockSpec((rows, HEAD_DIM), lambda c: (c, 0)),
                  _layer_spec(layer, (1, GROUP_WIDTH))],
        out_specs=pl.BlockSpec((rows, GROUP_WIDTH), lambda c: (c, 0)),
        out_shape=jax.ShapeDtypeStruct((S, GROUP_WIDTH), BF16),
        scratch_shapes=[pltpu.VMEM((N_HEADS, HEAD_DIM, HEAD_DIM), F32)],
        compiler_params=_cparams(("arbitrary",), 32),
    )(p32, p32, p16, p32, cos, sin, gains)


SB_TILE = 256
SB_HEADS = 2
SB_DEAD_LOG = -104.0


def _sb_tile(q, k_ref, v_ref, hs, j, run, acc, upper, diag, on=None):
    T = SB_TILE
    start = pl.multiple_of(j * T, T)
    k = k_ref[pl.ds(start, T), hs]
    v = v_ref[pl.ds(start, T), hs]
    z = _dot_nt(q, k) * ATTN_SCALE
    sp = jnp.maximum(z, 0.0) + jnp.log(1.0 + jnp.exp(-jnp.abs(z)))
    log_1m = -sp
    strict = on
    if diag:
        row = lax.broadcasted_iota(jnp.int32, (T, T), 0)
        col = lax.broadcasted_iota(jnp.int32, (T, T), 1)
        strict = col < row
    if strict is not None:
        log_1m = jnp.where(strict, log_1m, 0.0)
    hi = log_1m.astype(BF16)
    lo = (log_1m - hi.astype(F32)).astype(BF16)
    both = _dot(jnp.concatenate([hi, lo], axis=0), upper)
    between = both[:T] + both[T:] + run
    w = jnp.exp((z - sp) + between)
    if strict is not None:
        w = jnp.where(strict, w, 0.0)
    acc = acc + _dot(w.astype(BF16), v)
    run = between[:, 0:1] + log_1m[:, 0:1]
    return run, acc


def _sb_kernel(q_ref, k_ref, v_ref, o_ref):
    T = SB_TILE
    i = pl.program_id(1)
    r = lax.broadcasted_iota(jnp.int32, (T, T), 0)
    c = lax.broadcasted_iota(jnp.int32, (T, T), 1)
    upper = jnp.where(r > c, 1.0, 0.0).astype(BF16)

    def alive(run):
        return (jnp.max(run) > SB_DEAD_LOG).astype(jnp.int32)

    heads = []
    for h in range(SB_HEADS):
        hs = slice(h * HEAD_DIM, (h + 1) * HEAD_DIM)
        q = q_ref[:, hs]
        run = jnp.zeros((T, 1), F32)
        acc = jnp.zeros((T, HEAD_DIM), F32)
        run, acc = _sb_tile(q, k_ref, v_ref, hs, i, run, acc, upper, True)
        run, acc = _sb_tile(q, k_ref, v_ref, hs, jnp.maximum(i - 1, 0), run, acc, upper, False, on=i > 0)
        heads.append((hs, q, run, acc))

    for hs, q, run, acc in heads:
        def cond(carry):
            step, live, _, _ = carry
            return jnp.logical_and(step < i, live > 0)

        def body(carry, hs=hs, q=q):
            step, _, run, acc = carry
            run, acc = _sb_tile(q, k_ref, v_ref, hs, i - 1 - step, run, acc, upper, False)
            return step + 1, alive(run), run, acc

        _, _, _, acc = lax.while_loop(cond, body, (jnp.int32(1), alive(run), run, acc))
        o_ref[:, hs] = acc.astype(o_ref.dtype)


def _stick_breaking(p16, S):
    T = SB_TILE
    W = SB_HEADS * HEAD_DIM
    return pl.pallas_call(
        _sb_kernel,
        grid=(N_HEADS // SB_HEADS, S // T),
        in_specs=[pl.BlockSpec((T, W), lambda h, i: (i, COL_SQ // SB_HEADS + h)),
                  pl.BlockSpec((S, W), lambda h, i: (0, COL_SK // SB_HEADS + h)),
                  pl.BlockSpec((S, W), lambda h, i: (0, COL_SV // SB_HEADS + h))],
        out_specs=pl.BlockSpec((T, W), lambda h, i: (i, h)),
        out_shape=jax.ShapeDtypeStruct((S, GROUP_WIDTH), BF16),
        compiler_params=_cparams(("parallel", "arbitrary"), 40),
    )(p16, p16, p16)


_HG_LEVELS = (32, 16, 8, 4, 2, 1)
HGRN_STEP = 2


def _hgrn_constants():
    C = HGRN_CHUNK
    n = np.arange(C)
    seg = np.zeros((8 * C, C), np.float32)
    pair = np.zeros((7, C, C), np.float32)
    for l, s in enumerate(_HG_LEVELS):
        blk = n // s
        odd = (blk % 2) == 1
        a = blk * s
        e = a + s - 1
        j = n[None, :]
        t_odd = (j >= a[:, None]) & (j <= n[:, None])
        t_even = (j > n[:, None]) & (j <= e[:, None])
        seg[l * C:(l + 1) * C] = np.where(odd[:, None], t_odd, t_even)
        pair[l] = odd[:, None] & (blk[None, :] == blk[:, None] - 1)
    seg[6 * C:7 * C] = n[None, :] <= n[:, None]
    seg[7 * C:8 * C] = n[None, :] > n[:, None]
    pair[6] = np.eye(C)
    return seg, pair


def _hgrn_kernel(layer, q_ref, f_ref, v_ref, g_ref, lbl_ref, gain_ref, seg_ref, pair_ref, o_ref, st_ref):
    C = HGRN_CHUNK

    @pl.when(pl.program_id(0) == 0)
    def _():
        st_ref[...] = jnp.zeros_like(st_ref)

    logits = lbl_ref[...]
    e = jnp.exp(logits - jnp.max(logits, axis=0, keepdims=True))
    p = e / jnp.sum(e, axis=0, keepdims=True)
    lb = jnp.zeros((1, GROUP_WIDTH), F32)
    for l in range(1, layer + 1):
        lb = lb + p[l:l + 1, :]

    f = lb + (1.0 - lb) * jax.nn.sigmoid(f_ref[...])
    log_f = jnp.log(jnp.maximum(f, MIN_FORGET))
    kk = 1.0 - f
    parts = _split3(log_f)
    zero = jnp.zeros((C, GROUP_WIDTH), BF16)
    rhs = jnp.concatenate(
        [jnp.concatenate([p[c * C:(c + 1) * C] for p in parts] + [zero], axis=0) for c in range(HGRN_STEP)],
        axis=1)
    ex = jnp.exp(_dot(seg_ref[...], rhs))

    for h in range(N_HEADS):
        hs = slice(h * HEAD_DIM, (h + 1) * HEAD_DIM)
        st = st_ref[h]
        for c in range(HGRN_STEP):
            rs = slice(c * C, (c + 1) * C)
            es = slice(c * GROUP_WIDTH + h * HEAD_DIM, c * GROUP_WIDTH + (h + 1) * HEAD_DIM)
            q = q_ref[rs, hs] * ATTN_SCALE
            k = kk[rs, hs]
            v = v_ref[rs, hs]
            s = pair_ref[6] * _dot_nt(q.astype(BF16), k.astype(BF16))
            for l in range(6):
                el = ex[l * C:(l + 1) * C, es]
                s = s + pair_ref[l] * _dot_nt((q * el).astype(BF16), (k * el).astype(BF16))
            e_cum = ex[6 * C:7 * C, es]
            e_rest = ex[7 * C:8 * C, es]
            o = _dot(s.astype(BF16), v) + _dot_nt((q * e_cum).astype(BF16), st.astype(BF16))
            st = st * e_cum[C - 1:C, :] + _dot_tn(v, (k * e_rest).astype(BF16))
            y = o * lax.rsqrt(jnp.mean(o * o, axis=-1, keepdims=True) + NORM_EPS) * gain_ref[:, hs]
            o_ref[rs, hs] = (y * _silu(g_ref[rs, hs])).astype(o_ref.dtype)
        st_ref[h] = st


def _hgrn2(p32, p16, lb_logits, gains, layer, S):
    C = HGRN_CHUNK
    rows = C * HGRN_STEP
    gw = GROUP_WIDTH // LANES
    seg, pair = _hgrn_constants()
    seg = np.concatenate([seg, seg, seg, np.zeros_like(seg)], axis=1)
    return pl.pallas_call(
        functools.partial(_hgrn_kernel, layer),
        grid=(S // rows,),
        in_specs=[pl.BlockSpec((rows, GROUP_WIDTH), lambda c: (c, COL_GQ // gw)),
                  pl.BlockSpec((rows, GROUP_WIDTH), lambda c: (c, COL_GF // gw)),
                  pl.BlockSpec((rows, GROUP_WIDTH), lambda c: (c, COL_GI // gw)),
                  pl.BlockSpec((rows, GROUP_WIDTH), lambda c: (c, COL_GG // gw)),
                  pl.BlockSpec((DEPTH, GROUP_WIDTH), lambda c: (0, 0)),
                  _layer_spec(layer, (1, GROUP_WIDTH)),
                  pl.BlockSpec((8 * C, 4 * C), lambda c: (0, 0)),
                  pl.BlockSpec((7, C, C), lambda c: (0, 0, 0))],
        out_specs=pl.BlockSpec((rows, GROUP_WIDTH), lambda c: (c, 0)),
        out_shape=jax.ShapeDtypeStruct((S, GROUP_WIDTH), BF16),
        scratch_shapes=[pltpu.VMEM((N_HEADS, HEAD_DIM, HEAD_DIM), F32)],
        compiler_params=_cparams(("arbitrary",), 32),
    )(p32, p32, p16, p32, lb_logits, gains,
      jnp.asarray(seg, BF16), jnp.asarray(pair, F32))


NSA_Q_SCALE = ATTN_SCALE * math.log2(math.e)
NSA_MAX_SLC = LANES


def _nsa_prep_kernel(q_ref, kc_ref, vc_ref, ks_ref, kw_ref, vs_ref, vw_ref, cos_ref, sin_ref,
                     qo_ref, kco_ref, vco_ref, kso_ref, kwo_ref, vso_ref, vwo_ref):
    tm = q_ref.shape[0]
    ones = jnp.ones((tm, HEAD_DIM), BF16)
    vso_ref[:, 0:HEAD_DIM] = vs_ref[...]
    vso_ref[:, HEAD_DIM:] = ones
    vwo_ref[:, 0:HEAD_DIM] = vw_ref[...]
    vwo_ref[:, HEAD_DIM:] = ones
    cos = cos_ref[...]
    sin = sin_ref[...]
    for h in range(N_HEADS):
        hs = slice(h * HEAD_DIM, (h + 1) * HEAD_DIM)
        qo_ref[:, hs] = (_rope(q_ref[:, hs], cos, sin) * NSA_Q_SCALE).astype(BF16)
    kco_ref[...] = _rope(kc_ref[...], cos, sin).astype(BF16)
    kwo_ref[...] = _rope(kw_ref[...], cos, sin).astype(BF16)
    vco_ref[...] = vc_ref[...]
    kso_ref[:, 0:HEAD_DIM] = _rope(ks_ref[...], cos, sin).astype(BF16)
    tok = pl.program_id(0) * tm + lax.broadcasted_iota(jnp.int32, (tm, NSA_MAX_SLC), 0)
    lane = lax.broadcasted_iota(jnp.int32, (tm, NSA_MAX_SLC), 1)
    kso_ref[:, HEAD_DIM:] = jnp.where(jnp.right_shift(tok, NSA_SLC_SHIFT) == lane, 1.0, 0.0).astype(BF16)


def _nsa_prep(p32, p16, cos, sin, S):
    tm = min(S, 512)
    gw = GROUP_WIDTH // LANES
    row = lambda col: pl.BlockSpec((tm, HEAD_DIM), lambda i: (i, col))
    out = pl.BlockSpec((tm, HEAD_DIM), lambda i: (i, 0))
    aug = pl.BlockSpec((tm, HEAD_DIM + NSA_MAX_SLC), lambda i: (i, 0))
    return pl.pallas_call(
        _nsa_prep_kernel,
        grid=(S // tm,),
        in_specs=[pl.BlockSpec((tm, GROUP_WIDTH), lambda i: (i, COL_NQ // gw)),
                  row(COL_KC), row(COL_VC), row(COL_KS), row(COL_KW), row(COL_VS), row(COL_VW), out, out],
        out_specs=[pl.BlockSpec((tm, GROUP_WIDTH), lambda i: (i, 0)), out, out, aug, out, aug, aug],
        out_shape=[jax.ShapeDtypeStruct((S, GROUP_WIDTH), BF16),
                   jax.ShapeDtypeStruct((S, HEAD_DIM), BF16),
                   jax.ShapeDtypeStruct((S, HEAD_DIM), BF16),
                   jax.ShapeDtypeStruct((S, HEAD_DIM + NSA_MAX_SLC), BF16),
                   jax.ShapeDtypeStruct((S, HEAD_DIM), BF16),
                   jax.ShapeDtypeStruct((S, 2 * HEAD_DIM), BF16),
                   jax.ShapeDtypeStruct((S, 2 * HEAD_DIM), BF16)],
        compiler_params=_cparams(("parallel",), 32),
    )(p32, p32, p16, p32, p32, p16, p16, cos, sin)


def _nsa_compress_kernel(kb_ref, vb_ref, wk_ref, wv_ref, pk_ref, pv_ref, ko_ref, vo_ref):
    nblk = kb_ref.shape[0]
    half = NSA_CMP_STRIDE * HEAD_DIM
    rowid = lax.broadcasted_iota(jnp.int32, (nblk, HEAD_DIM), 0)
    for b_ref, w_ref, p_ref, o_ref in ((kb_ref, wk_ref, pk_ref, ko_ref), (vb_ref, wv_ref, pv_ref, vo_ref)):
        w = w_ref[...]
        top = _dot(b_ref[...], w[:half])
        bot = _dot(b_ref[...], w[half:])
        ph, pm, plo = _split3(p_ref[...])
        const = _dot(ph, w) + _dot(pm, w) + _dot(plo, w)
        shifted = pltpu.roll(bot, nblk - 1, axis=0)
        out = top + shifted + const
        o_ref[...] = jnp.where(rowid < nblk - 1, out, 0.0).astype(BF16)


def _nsa_compress(kc16, vc16, wk16, wv16, pos_k, pos_v, layer, S):
    nblk = S // NSA_CMP_STRIDE
    width = NSA_CMP_STRIDE * HEAD_DIM
    kb = kc16.reshape(nblk, width)
    vb = vc16.reshape(nblk, width)
    full = lambda shape: pl.BlockSpec(shape, lambda i: (0,) * len(shape))
    return pl.pallas_call(
        _nsa_compress_kernel,
        grid=(1,),
        in_specs=[full((nblk, width)), full((nblk, width)),
                  _layer_spec(layer, (2 * width, HEAD_DIM)), _layer_spec(layer, (2 * width, HEAD_DIM)),
                  _layer_spec(layer, (1, 2 * width)), _layer_spec(layer, (1, 2 * width))],
        out_specs=[full((nblk, HEAD_DIM)), full((nblk, HEAD_DIM))],
        out_shape=[jax.ShapeDtypeStruct((nblk, HEAD_DIM), BF16)] * 2,
        compiler_params=_cparams(("arbitrary",), 32),
    )(kb, vb, wk16, wv16, pos_k, pos_v)


NSA_Q = 256
NSA_KT = 512
NSA_ROWS = N_HEADS * NSA_Q
NSA_OFF = -(2.0 ** 126)


def _lanes4(x):
    return jnp.maximum(jnp.maximum(x[:, 0:LANES], x[:, LANES:2 * LANES]),
                       jnp.maximum(x[:, 2 * LANES:3 * LANES], x[:, 3 * LANES:4 * LANES]))


def _nsa_attn_kernel(n_slc, q_ref, g_ref, kcmp_ref, vcmp_ref, ovt_ref, ks_ref, vs_ref, kw_ref, vw_ref,
                     wup_ref, wdn_ref, o_ref, wup16_ref, wdn16_ref, m_ref, acc_ref):
    Q, KT, R = NSA_Q, NSA_KT, NSA_ROWS
    nsl = NSA_MAX_SLC
    bi = pl.program_id(0)
    wup16_ref[...] = wup_ref[...].astype(BF16)
    wdn16_ref[...] = wdn_ref[...].astype(BF16)
    q0 = bi * Q
    ncmp = kcmp_ref.shape[0]
    qs = jnp.concatenate([q_ref[:, h * HEAD_DIM:(h + 1) * HEAD_DIM] for h in range(N_HEADS)], axis=0)
    qpos_r = q0 + jnp.bitwise_and(lax.broadcasted_iota(jnp.int32, (R, 1), 0), Q - 1)

    last_vis = jnp.minimum(jnp.right_shift(qpos_r - (NSA_CMP_LEN - 1), 4), ncmp - 2)
    cmp_mask = lax.broadcasted_iota(jnp.int32, (R, ncmp), 1) <= last_vis
    s2 = jnp.where(cmp_mask, _dot_nt(qs, kcmp_ref[...]), MASK_VALUE)
    pc = jnp.exp2(s2 - jnp.max(s2, axis=-1, keepdims=True))
    inv = jnp.where(last_vis >= 0, 1.0 / jnp.sum(pc, axis=-1, keepdims=True), 0.0)
    pc = pc * inv
    o_cmp = _dot(pc.astype(BF16), vcmp_ref[...])

    W = NSA_WINDOW + Q
    ws = pl.multiple_of(jnp.maximum(q0 - NSA_WINDOW, 0), Q)
    dist = (qpos_r - ws) - lax.broadcasted_iota(jnp.int32, (R, W), 1)
    wmask = pltpu.bitcast(dist, jnp.uint32) < NSA_WINDOW
    s2 = jnp.where(wmask, _dot_nt(qs, kw_ref[pl.ds(ws, W), :]), MASK_VALUE)
    pw = jnp.exp2(s2 - jnp.max(s2, axis=-1, keepdims=True))
    ow = _dot(pw.astype(BF16), vw_ref[pl.ds(ws, W), :])
    o_win = ow[:, 0:HEAD_DIM] / ow[:, HEAD_DIM:]

    p_sum = pc[0:Q] + pc[Q:2 * Q] + pc[2 * Q:3 * Q] + pc[3 * Q:4 * Q]
    ph = p_sum.astype(BF16)
    plo = (p_sum - ph.astype(F32)).astype(BF16)
    imp_t = _dot_nt(ovt_ref[...], ph) + _dot_nt(ovt_ref[...], plo)
    blk = lax.broadcasted_iota(jnp.int32, (nsl, Q), 0)
    blkf = blk.astype(F32)
    qpos_l = q0 + lax.broadcasted_iota(jnp.int32, (nsl, Q), 1)
    cur = jnp.right_shift(qpos_l, NSA_SLC_SHIFT)
    forced = (blk == 0) | (blk == cur) | (blk == cur - 1)
    future = blk * NSA_SLC_LEN > qpos_l
    work = jnp.where(future, -1.0, jnp.where(forced, FORCED_SCORE, imp_t))
    work = jnp.where(blk < n_slc, work, -jnp.inf)
    sel = jnp.zeros((nsl, Q), F32)
    for _ in range(min(NSA_TOPK, n_slc)):
        mx = jnp.max(work, axis=0, keepdims=True)
        first = jnp.min(jnp.where(work == mx, blkf, float(nsl)), axis=0, keepdims=True)
        hit = blkf == first
        sel = jnp.where(hit, 1.0, sel)
        work = jnp.where(hit, -jnp.inf, work)
    bias_t = jnp.where(future, NSA_OFF, jnp.where(sel > 0.0, 0.0, NSA_OFF))
    bias = bias_t.T.astype(BF16)
    q_aug = jnp.concatenate([qs, jnp.concatenate([bias] * N_HEADS, axis=0)], axis=1)

    n_tiles = (q0 + Q + KT - 1) // KT

    def scores(t, width, causal):
        start = pl.multiple_of(t * KT, KT)
        s2 = _dot_nt(q_aug, ks_ref[pl.ds(start, width), :])
        if causal:
            kpos = start + lax.broadcasted_iota(jnp.int32, (R, width), 1)
            s2 = jnp.where(kpos <= qpos_r, s2, NSA_OFF)
        return start, s2

    def max_tile(t, width, causal):
        _, s2 = scores(t, width, causal)
        m = m_ref[...]
        for c in range(width // KT):
            m = jnp.maximum(m, _lanes4(s2[:, c * KT:(c + 1) * KT]))
        m_ref[...] = m

    def sum_tile(t, width, causal):
        start, s2 = scores(t, width, causal)
        pt = jnp.exp2(s2 - jnp.concatenate([m_ref[...]] * (width // LANES), axis=1))
        acc_ref[...] += _dot(pt.astype(BF16), vs_ref[pl.ds(start, width), :])

    def loop(fn):
        n_full = n_tiles - 1

        def body(u, carry):
            fn(2 * u, 2 * KT, False)
            return carry

        lax.fori_loop(0, n_full // 2, body, 0)

        @pl.when(n_full % 2 == 1)
        def _():
            fn(n_full - 1, KT, False)

        fn(n_tiles - 1, KT, True)

    m_ref[...] = jnp.full_like(m_ref, NSA_OFF)
    loop(max_tile)
    m_ref[...] = jnp.broadcast_to(jnp.max(m_ref[...], axis=-1, keepdims=True), m_ref.shape)
    acc_ref[...] = jnp.zeros_like(acc_ref)
    loop(sum_tile)
    o_slc = acc_ref[:, 0:HEAD_DIM] / acc_ref[:, HEAD_DIM:]

    gate = jax.nn.sigmoid(g_ref[...])
    for h in range(N_HEADS):
        rows = slice(h * Q, (h + 1) * Q)
        out = (gate[:, 3 * h:3 * h + 1] * o_cmp[rows] + gate[:, 3 * h + 1:3 * h + 2] * o_slc[rows]
               + gate[:, 3 * h + 2:3 * h + 3] * o_win[rows])
        o_ref[:, h * HEAD_DIM:(h + 1) * HEAD_DIM] = out.astype(o_ref.dtype)


def _nsa_overlap(S):
    n_cmp = (S - NSA_CMP_LEN) // NSA_CMP_STRIDE + 1
    n_slc = S // NSA_SLC_LEN
    nblk = S // NSA_CMP_STRIDE
    nsl = NSA_MAX_SLC
    assert n_slc <= nsl
    cmp_start = np.arange(n_cmp) * NSA_CMP_STRIDE
    slc_start = np.arange(n_slc) * NSA_SLC_LEN
    ov = (np.minimum(cmp_start[:, None] + NSA_CMP_LEN, slc_start[None, :] + NSA_SLC_LEN)
          - np.maximum(cmp_start[:, None], slc_start[None, :]))
    out = np.zeros((nblk, nsl), np.float32)
    out[:n_cmp, :n_slc] = np.clip(ov, 0, None) / NSA_CMP_LEN
    return out, n_slc


def _nsa_attention(qn16, p32, kcmp, vcmp, ks16, vs16, kw16, vw16, w_up, w_down, layer, S):
    Q = NSA_Q
    steps = S // Q
    ov, n_slc = _nsa_overlap(S)
    nblk, nsl = ov.shape
    up_rows, dn_rows = w_up.shape[1] // steps, w_down.shape[1] // steps
    full = lambda shape: pl.BlockSpec(shape, lambda i: (0,) * len(shape))
    return pl.pallas_call(
        functools.partial(_nsa_attn_kernel, n_slc),
        grid=(steps,),
        in_specs=[pl.BlockSpec((Q, GROUP_WIDTH), lambda i: (i, 0)),
                  pl.BlockSpec((Q, HEAD_DIM), lambda i: (i, COL_NG)),
                  full((nblk, HEAD_DIM)), full((nblk, HEAD_DIM)), full((nsl, nblk)),
                  full((S, HEAD_DIM + NSA_MAX_SLC)), full((S, 2 * HEAD_DIM)),
                  full((S, HEAD_DIM)), full((S, 2 * HEAD_DIM)),
                  _layer_spec(layer, (up_rows, w_up.shape[2]), lambda i: (i, 0)),
                  _layer_spec(layer, (dn_rows, w_down.shape[2]), lambda i: (i, 0))],
        out_specs=[pl.BlockSpec((Q, GROUP_WIDTH), lambda i: (i, 0)),
                   pl.BlockSpec((None, up_rows, w_up.shape[2]), lambda i: (0, i, 0)),
                   pl.BlockSpec((None, dn_rows, w_down.shape[2]), lambda i: (0, i, 0))],
        out_shape=[jax.ShapeDtypeStruct((S, GROUP_WIDTH), BF16),
                   jax.ShapeDtypeStruct((1,) + w_up.shape[1:], BF16),
                   jax.ShapeDtypeStruct((1,) + w_down.shape[1:], BF16)],
        scratch_shapes=[pltpu.VMEM((NSA_ROWS, LANES), F32), pltpu.VMEM((NSA_ROWS, 2 * HEAD_DIM), F32)],
        compiler_params=_cparams(("arbitrary",), 56),
    )(qn16, p32, kcmp, vcmp, jnp.asarray(ov.T, BF16), ks16, vs16, kw16, vw16, w_up, w_down)


def _outproj_kernel(a0_ref, a1_ref, a2_ref, a3_ref, w_ref, x_ref, o_ref):
    acc = x_ref[...]
    for g, a_ref in enumerate((a0_ref, a1_ref, a2_ref, a3_ref)):
        acc = acc + _dot(a_ref[...], w_ref[g * GROUP_WIDTH:(g + 1) * GROUP_WIDTH, :])
    o_ref[...] = acc


def _outproj(parts, w16, layer, x, tm):
    S, N = x.shape
    a_spec = pl.BlockSpec((tm, GROUP_WIDTH), lambda i: (i, 0))
    return pl.pallas_call(
        _outproj_kernel,
        grid=(S // tm,),
        in_specs=[a_spec, a_spec, a_spec, a_spec,
                  _layer_spec(layer, (4 * GROUP_WIDTH, N)),
                  pl.BlockSpec((tm, N), lambda i: (i, 0))],
        out_specs=pl.BlockSpec((tm, N), lambda i: (i, 0)),
        out_shape=jax.ShapeDtypeStruct((S, N), F32),
        compiler_params=_cparams(("parallel",), 48),
    )(*parts, w16, x)


def _xattn_kernel(x_ref, g_ref, wq_ref, k_ref, v_ref, wo_ref, o_ref):
    x = x_ref[...]
    h = _rms(x, g_ref[...]).astype(BF16)
    q = _dot(h, wq_ref[...]).astype(BF16)
    outs = []
    for hd in range(N_HEADS):
        hs = slice(hd * HEAD_DIM, (hd + 1) * HEAD_DIM)
        s = _dot_nt(q[:, hs], k_ref[:, hs]) * ATTN_SCALE
        m = jnp.max(s, axis=-1, keepdims=True)
        p = jnp.exp(s - m)
        p = p / jnp.sum(p, axis=-1, keepdims=True)
        outs.append(_dot(p.astype(BF16), v_ref[:, hs]).astype(BF16))
    o = jnp.concatenate(outs, axis=-1)
    o_ref[...] = x + _dot(o, wo_ref[...])


def _xattn(x, gains, wq16, k16, v16, wo16, layer, tm):
    S, K = x.shape
    XW = wq16.shape[2]
    M = k16.shape[0]
    return pl.pallas_call(
        _xattn_kernel,
        grid=(S // tm,),
        in_specs=[pl.BlockSpec((tm, K), lambda i: (i, 0)),
                  _layer_spec(layer, (1, K)),
                  _layer_spec(layer, (K, XW)),
                  pl.BlockSpec((M, XW), lambda i: (0, 0)),
                  pl.BlockSpec((M, XW), lambda i: (0, 0)),
                  _layer_spec(layer, (XW, K))],
        out_specs=pl.BlockSpec((tm, K), lambda i: (i, 0)),
        out_shape=jax.ShapeDtypeStruct((S, K), F32),
        compiler_params=_cparams(("parallel",), 48),
    )(x, gains, wq16, k16, v16, wo16)


FFN_HALO = 16
FFN_SUB = 256


def _ffn_kernel(final, x_ref, xp_ref, g_ref, wg_ref, wv_ref, cwg_ref, cwv_ref, cbg_ref, cbv_ref, wd_ref,
                fg_ref, o_ref, h_ref):
    i = pl.program_id(0)
    f = pl.program_id(1)
    tf = wg_ref.shape[1]

    @pl.when(f == 0)
    def _():
        hp = _rms(xp_ref[...], g_ref[...])
        h_ref[0:FFN_HALO, :] = jnp.where(i > 0, hp, 0.0).astype(BF16)
        h_ref[FFN_HALO:, :] = _rms(x_ref[...], g_ref[...]).astype(BF16)
        o_ref[...] = x_ref[...]

    h = h_ref[...]

    def conv(w_ref, cw_ref, cb_ref, cols):
        u = _dot(h, w_ref[:, cols])
        u1 = pltpu.roll(u, 1, axis=0)
        u2 = pltpu.roll(u, 2, axis=0)
        c = cb_ref[:, cols] + cw_ref[2:3, cols] * u + cw_ref[1:2, cols] * u1 + cw_ref[0:1, cols] * u2
        return c[FFN_HALO:, :]

    acts = []
    for s in range(tf // FFN_SUB):
        cols = slice(s * FFN_SUB, (s + 1) * FFN_SUB)
        gate = conv(wg_ref, cwg_ref, cbg_ref, cols)
        val = conv(wv_ref, cwv_ref, cbv_ref, cols)
        acts.append((_silu(gate) * val).astype(BF16))
    o_ref[...] += _dot(jnp.concatenate(acts, axis=1), wd_ref[...])

    if final:
        @pl.when(f == pl.num_programs(1) - 1)
        def _():
            o_ref[...] = _rms(o_ref[...], fg_ref[...])


def _ffn(x, gains, w_up16, conv_w, conv_b, w_down16, final_gain, layer, tm, tf):
    S, K = x.shape
    F = w_down16.shape[1]
    nf = F // tf
    hb = tm // FFN_HALO
    return pl.pallas_call(
        functools.partial(_ffn_kernel, layer == DEPTH - 1),
        grid=(S // tm, nf),
        in_specs=[pl.BlockSpec((tm, K), lambda i, f: (i, 0), pipeline_mode=pl.Buffered(1)),
                  pl.BlockSpec((FFN_HALO, K), lambda i, f: (jnp.maximum(i * hb - 1, 0), 0)),
                  _layer_spec(layer, (1, K)),
                  _layer_spec(0, (K, tf), lambda i, f: (0, f)),
                  _layer_spec(0, (K, tf), lambda i, f: (0, nf + f)),
                  _layer_spec(layer, (3, tf), lambda i, f: (0, f)),
                  _layer_spec(layer, (3, tf), lambda i, f: (0, nf + f)),
                  _layer_spec(layer, (1, tf), lambda i, f: (0, f)),
                  _layer_spec(layer, (1, tf), lambda i, f: (0, nf + f)),
                  _layer_spec(0, (tf, K), lambda i, f: (f, 0)),
                  pl.BlockSpec((1, K), lambda i, f: (0, 0))],
        out_specs=pl.BlockSpec((tm, K), lambda i, f: (i, 0)),
        out_shape=jax.ShapeDtypeStruct((S, K), F32),
        scratch_shapes=[pltpu.VMEM((tm + FFN_HALO, K), BF16)],
        compiler_params=_cparams(("parallel", "arbitrary"), 56),
    )(x, x, gains, w_up16, w_up16, conv_w, conv_w, conv_b, conv_b, w_down16, final_gain)


def kernel(x, mem, positions, mix_norm, w_in, ret_norm, hgrn_lb_logits, hgrn_norm, nsa_pos_k, nsa_pos_v,
           nsa_w_ck, nsa_w_cv, w_out, xattn_norm, mem_norm, xattn_wq, xattn_wk, xattn_wv, xattn_wo,
           ffn_norm, ffn_w_up, ffn_conv_w, ffn_conv_b, ffn_w_down, final_norm):
    B, S, _ = x.shape
    assert B == 1 and S % 1024 == 0
    L = DEPTH
    xs = x.reshape(S, D_MODEL)
    mem2 = mem.reshape(N_MEM, D_MODEL)
    row = lambda g: g.reshape(L, 1, g.shape[-1])
    w_in16 = jnp.pad(w_in.astype(BF16), ((0, 0), (0, 0), (0, IN_COLS_PAD - IN_COLS)))
    w_out16 = w_out.astype(BF16)
    wq16, wk16, wv16, wo16 = (w.astype(BF16) for w in (xattn_wq, xattn_wk, xattn_wv, xattn_wo))
    wck16 = nsa_w_ck.reshape(L, NSA_CMP_LEN * HEAD_DIM, HEAD_DIM).astype(BF16)
    wcv16 = nsa_w_cv.reshape(L, NSA_CMP_LEN * HEAD_DIM, HEAD_DIM).astype(BF16)
    pos_k = nsa_pos_k.reshape(L, 1, NSA_CMP_LEN * HEAD_DIM)
    pos_v = nsa_pos_v.reshape(L, 1, NSA_CMP_LEN * HEAD_DIM)
    mix_g, ret_g, hgrn_g, xattn_g, mem_g, ffn_g = (
        row(g) for g in (mix_norm, ret_norm, hgrn_norm, xattn_norm, mem_norm, ffn_norm))
    conv_b = row(ffn_conv_b)
    final_g = final_norm.reshape(1, D_MODEL)

    cos, sin = _rope_tables(positions, S)
    for layer in range(L):
        p32, p16 = _inproj(xs, mix_g, w_in16, layer, PROJ_TM, PROJ_TN)
        o_ret = _retention(p32, p16, cos, sin, ret_g, layer, S)
        o_sb = _stick_breaking(p16, S)
        o_hg = _hgrn2(p32, p16, hgrn_lb_logits, hgrn_g, layer, S)
        qn16, kc16, vc16, ks16, kw16, vs16, vw16 = _nsa_prep(p32, p16, cos, sin, S)
        kcmp, vcmp = _nsa_compress(kc16, vc16, wck16, wcv16, pos_k, pos_v, layer, S)
        o_nsa, w_up16, w_down16 = _nsa_attention(qn16, p32, kcmp, vcmp, ks16, vs16, kw16, vw16,
                                                 ffn_w_up, ffn_w_down, layer, S)
        xs = _outproj((o_ret, o_sb, o_hg, o_nsa), w_out16, layer, xs, ROW_TM)
        k16 = _normproj(mem2, mem_g, wk16, layer, BF16)
        v16 = _normproj(mem2, mem_g, wv16, layer, BF16)
        xs = _xattn(xs, xattn_g, wq16, k16, v16, wo16, layer, ROW_TM)
        xs = _ffn(xs, ffn_g, w_up16, ffn_conv_w, conv_b, w_down16, final_g, layer, FFN_TM, FFN_TF)
    return xs.reshape(B, S, D_MODEL)
```

```python
import functools
import math

import numpy as np
import jax
import jax.numpy as jnp
from jax import lax
from jax.experimental import pallas as pl
from jax.experimental.pallas import tpu as pltpu

F32 = jnp.float32
BF16 = jnp.bfloat16

D_MODEL = 2048
DEPTH = 4
N_HEADS = 4
HEAD_DIM = 128
GROUP_WIDTH = N_HEADS * HEAD_DIM
ROPE_THETA = 10000.0
RET_CHUNK = 128
HGRN_CHUNK = 64
NSA_CMP_LEN = 32
NSA_CMP_STRIDE = 16
NSA_SLC_LEN = 64
NSA_SLC_SHIFT = 6
NSA_TOPK = 16
NSA_WINDOW = 512
N_MEM = 256
D_FF = 5632
NORM_EPS = 1e-6
MASK_VALUE = -1e30
FORCED_SCORE = 1e4
MIN_FORGET = 1e-6
IN_COLS = 12 * GROUP_WIDTH + 6 * HEAD_DIM + 3 * N_HEADS
ATTN_SCALE = HEAD_DIM ** -0.5
LANES = 128


IN_COLS_PAD = 7168

COL_RQ, COL_RK, COL_RV, COL_RG = 0, 4, 8, 12
COL_SQ, COL_SK, COL_SV = 16, 20, 24
COL_GQ, COL_GF, COL_GI, COL_GG = 28, 32, 36, 40
COL_NQ = 44
COL_KC, COL_VC, COL_KS, COL_VS, COL_KW, COL_VW, COL_NG = 48, 49, 50, 51, 52, 53, 54

V7X_VMEM_BYTES = 64 * 1024 * 1024

PROJ_TM, PROJ_TN = 1024, 1024
ROW_TM = 512
FFN_TM, FFN_TF = 1024, 512


def _cparams(semantics, vmem_mb):
    return pltpu.CompilerParams(dimension_semantics=semantics,
                                vmem_limit_bytes=min(vmem_mb * 1024 * 1024, V7X_VMEM_BYTES - (8 << 20)))


def _layer_spec(layer, tail_shape, tail_index=None):
    if tail_index is None:
        tail_index = lambda *ids: (0,) * len(tail_shape)
    return pl.BlockSpec((None,) + tuple(tail_shape), lambda *ids: (layer,) + tuple(tail_index(*ids)))


def _dot(a, b):
    return jnp.dot(a, b, preferred_element_type=F32)


def _dot_nt(a, b):
    return lax.dot_general(a, b, (((1,), (1,)), ((), ())), preferred_element_type=F32)


def _dot_tn(a, b):
    return lax.dot_general(a, b, (((0,), (0,)), ((), ())), preferred_element_type=F32)


def _split3(x):
    h1 = x.astype(BF16)
    r1 = x - h1.astype(F32)
    h2 = r1.astype(BF16)
    r2 = r1 - h2.astype(F32)
    return h1, h2, r2.astype(BF16)


def _rms(x, gain):
    ms = jnp.mean(x * x, axis=-1, keepdims=True)
    return x * lax.rsqrt(ms + NORM_EPS) * gain


def _rope(x, cos, sin_signed):
    return x * cos + pltpu.roll(x, HEAD_DIM // 2, axis=1) * sin_signed


def _silu(x):
    return x * jax.nn.sigmoid(x)


def _rope_table_kernel(pos_ref, inv_ref, cos_ref, sin_ref):
    ang = pos_ref[...].astype(F32) * inv_ref[...]
    lane = lax.broadcasted_iota(jnp.int32, ang.shape, 1)
    cos_ref[...] = jnp.cos(ang)
    s = jnp.sin(ang)
    sin_ref[...] = jnp.where(lane < HEAD_DIM // 2, -s, s)


def _rope_tables(positions, S):
    half = HEAD_DIM // 2
    inv = ROPE_THETA ** (-jnp.arange(half, dtype=F32) / half)
    inv = jnp.concatenate([inv, inv]).reshape(1, HEAD_DIM)
    pos = positions.reshape(S, 1)
    tm = min(S, 1024)
    return pl.pallas_call(
        _rope_table_kernel,
        grid=(S // tm,),
        in_specs=[pl.BlockSpec((tm, 1), lambda i: (i, 0)),
                  pl.BlockSpec((1, HEAD_DIM), lambda i: (0, 0))],
        out_specs=[pl.BlockSpec((tm, HEAD_DIM), lambda i: (i, 0)),
                   pl.BlockSpec((tm, HEAD_DIM), lambda i: (i, 0))],
        out_shape=[jax.ShapeDtypeStruct((S, HEAD_DIM), F32)] * 2,
        compiler_params=_cparams(("arbitrary",), 32),
    )(pos, inv)


def _inproj_kernel(x_ref, g_ref, w_ref, o32_ref, o16_ref, h_ref):
    @pl.when(pl.program_id(1) == 0)
    def _():
        h_ref[...] = _rms(x_ref[...], g_ref[...]).astype(BF16)

    y = _dot(h_ref[...], w_ref[...])
    o32_ref[...] = y
    o16_ref[...] = y.astype(BF16)


def _inproj(x, gains, w16, layer, tm, tn):
    S, K = x.shape
    N = w16.shape[2]
    return pl.pallas_call(
        _inproj_kernel,
        grid=(S // tm, N // tn),
        in_specs=[pl.BlockSpec((tm, K), lambda i, j: (i, 0)),
                  _layer_spec(layer, (1, K)),
                  _layer_spec(layer, (K, tn), lambda i, j: (0, j))],
        out_specs=[pl.BlockSpec((tm, tn), lambda i, j: (i, j)),
                   pl.BlockSpec((tm, tn), lambda i, j: (i, j))],
        out_shape=[jax.ShapeDtypeStruct((S, N), F32), jax.ShapeDtypeStruct((S, N), BF16)],
        scratch_shapes=[pltpu.VMEM((tm, K), BF16)],
        compiler_params=_cparams(("parallel", "arbitrary"), 48),
    )(x, gains, w16)


def _normproj_kernel(x_ref, g_ref, w_ref, o_ref):
    h = _rms(x_ref[...], g_ref[...]).astype(BF16)
    o_ref[...] = _dot(h, w_ref[...]).astype(o_ref.dtype)


def _normproj(x, gains, w16, layer, out_dtype):
    M, K = x.shape
    N = w16.shape[2]
    return pl.pallas_call(
        _normproj_kernel,
        grid=(1,),
        in_specs=[pl.BlockSpec((M, K), lambda i: (0, 0)),
                  _layer_spec(layer, (1, K)),
                  _layer_spec(0, (K, N))],
        out_specs=pl.BlockSpec((M, N), lambda i: (0, 0)),
        out_shape=jax.ShapeDtypeStruct((M, N), out_dtype),
        compiler_params=_cparams(("arbitrary",), 32),
    )(x, gains, w16)


_RET_LOG_GAMMA = [math.log1p(-(2.0 ** (-5.0 - h))) for h in range(N_HEADS)]
RET_STEP = 4


def _ret_kernel(q_ref, k_ref, v_ref, g_ref, cos_ref, sin_ref, gain_ref, o_ref, st_ref):
    C = RET_CHUNK

    @pl.when(pl.program_id(0) == 0)
    def _():
        st_ref[...] = jnp.zeros_like(st_ref)

    row = lax.broadcasted_iota(jnp.int32, (C, C), 0)
    col = lax.broadcasted_iota(jnp.int32, (C, C), 1)
    rel = (row - col).astype(F32)
    rowf = row.astype(F32)
    for h in range(N_HEADS):
        hs = slice(h * HEAD_DIM, (h + 1) * HEAD_DIM)
        lg = _RET_LOG_GAMMA[h]
        decay = jnp.where(rel >= 0, jnp.exp(lg * jnp.maximum(rel, 0.0)), 0.0)
        q_w = jnp.exp(lg * (rowf + 1.0))
        k_w = jnp.exp(lg * (C - 1.0 - rowf))
        st = st_ref[h]
        for c in range(RET_STEP):
            rs = slice(c * C, (c + 1) * C)
            cos = cos_ref[rs, :]
            sin = sin_ref[rs, :]
            q = _rope(q_ref[rs, hs], cos, sin)
            k = _rope(k_ref[rs, hs], cos, sin) * ATTN_SCALE
            v = v_ref[rs, hs]
            s = _dot_nt(q.astype(BF16), k.astype(BF16)) * decay
            o = _dot(s.astype(BF16), v) + _dot_nt((q * q_w).astype(BF16), st.astype(BF16))
            st = st * math.exp(lg * C) + _dot_tn(v, (k * k_w).astype(BF16))
            oc = o - jnp.mean(o, axis=-1, keepdims=True)
            y = oc * lax.rsqrt(jnp.mean(oc * oc, axis=-1, keepdims=True) + NORM_EPS) * gain_ref[:, hs]
            o_ref[rs, hs] = (y * _silu(g_ref[rs, hs])).astype(o_ref.dtype)
        st_ref[h] = st


def _retention(p32, p16, cos, sin, gains, layer, S):
    rows = RET_CHUNK * RET_STEP
    gw = GROUP_WIDTH // LANES
    return pl.pallas_call(
        _ret_kernel,
        grid=(S // rows,),
        in_specs=[pl.BlockSpec((rows, GROUP_WIDTH), lambda c: (c, COL_RQ // gw)),
                  pl.BlockSpec((rows, GROUP_WIDTH), lambda c: (c, COL_RK // gw)),
                  pl.BlockSpec((rows, GROUP_WIDTH), lambda c: (c, COL_RV // gw)),
                  pl.BlockSpec((rows, GROUP_WIDTH), lambda c: (c, COL_RG // gw)),
                  pl.BlockSpec((rows, HEAD_DIM), lambda c: (c, 0)),
                  pl.BlockSpec((rows, HEAD_DIM), lambda c: (c, 0)),
                  _layer_spec(layer, (1, GROUP_WIDTH))],
        out_specs=pl.BlockSpec((rows, GROUP_WIDTH), lambda c: (c, 0)),
        out_shape=jax.ShapeDtypeStruct((S, GROUP_WIDTH), BF16),
        scratch_shapes=[pltpu.VMEM((N_HEADS, HEAD_DIM, HEAD_DIM), F32)],
        compiler_params=_cparams(("arbitrary",), 32),
    )(p32, p32, p16, p32, cos, sin, gains)


SB_TILE = 256
SB_HEADS = 2
SB_DEAD_LOG = -104.0


def _sb_tile(q, k_ref, v_ref, hs, j, run, acc, upper, diag, on=None):
    T = SB_TILE
    start = pl.multiple_of(j * T, T)
    k = k_ref[pl.ds(start, T), hs]
    v = v_ref[pl.ds(start, T), hs]
    z = _dot_nt(q, k) * ATTN_SCALE
    sp = jnp.maximum(z, 0.0) + jnp.log(1.0 + jnp.exp(-jnp.abs(z)))
    log_1m = -sp
    strict = on
    if diag:
        row = lax.broadcasted_iota(jnp.int32, (T, T), 0)
        col = lax.broadcasted_iota(jnp.int32, (T, T), 1)
        strict = col < row
    if strict is not None:
        log_1m = jnp.where(strict, log_1m, 0.0)
    hi = log_1m.astype(BF16)
    lo = (log_1m - hi.astype(F32)).astype(BF16)
    both = _dot(jnp.concatenate([hi, lo], axis=0), upper)
    between = both[:T] + both[T:] + run
    w = jnp.exp((z - sp) + between)
    if strict is not None:
        w = jnp.where(strict, w, 0.0)
    acc = acc + _dot(w.astype(BF16), v)
    run = between[:, 0:1] + log_1m[:, 0:1]
    return run, acc


def _sb_kernel(q_ref, k_ref, v_ref, o_ref):
    T = SB_TILE
    i = pl.program_id(1)
    r = lax.broadcasted_iota(jnp.int32, (T, T), 0)
    c = lax.broadcasted_iota(jnp.int32, (T, T), 1)
    upper = jnp.where(r > c, 1.0, 0.0).astype(BF16)

    def alive(run):
        return (jnp.max(run) > SB_DEAD_LOG).astype(jnp.int32)

    heads = []
    for h in range(SB_HEADS):
        hs = slice(h * HEAD_DIM, (h + 1) * HEAD_DIM)
        q = q_ref[:, hs]
        run = jnp.zeros((T, 1), F32)
        acc = jnp.zeros((T, HEAD_DIM), F32)
        run, acc = _sb_tile(q, k_ref, v_ref, hs, i, run, acc, upper, True)
        run, acc = _sb_tile(q, k_ref, v_ref, hs, jnp.maximum(i - 1, 0), run, acc, upper, False, on=i > 0)
        heads.append((hs, q, run, acc))

    for hs, q, run, acc in heads:
        def cond(carry):
            step, live, _, _ = carry
            return jnp.logical_and(step < i, live > 0)

        def body(carry, hs=hs, q=q):
            step, _, run, acc = carry
            run, acc = _sb_tile(q, k_ref, v_ref, hs, i - 1 - step, run, acc, upper, False)
            return step + 1, alive(run), run, acc

        _, _, _, acc = lax.while_loop(cond, body, (jnp.int32(1), alive(run), run, acc))
        o_ref[:, hs] = acc.astype(o_ref.dtype)


def _stick_breaking(p16, S):
    T = SB_TILE
    W = SB_HEADS * HEAD_DIM
    return pl.pallas_call(
        _sb_kernel,
        grid=(N_HEADS // SB_HEADS, S // T),
        in_specs=[pl.BlockSpec((T, W), lambda h, i: (i, COL_SQ // SB_HEADS + h)),
                  pl.BlockSpec((S, W), lambda h, i: (0, COL_SK // SB_HEADS + h)),
                  pl.BlockSpec((S, W), lambda h, i: (0, COL_SV // SB_HEADS + h))],
        out_specs=pl.BlockSpec((T, W), lambda h, i: (i, h)),
        out_shape=jax.ShapeDtypeStruct((S, GROUP_WIDTH), BF16),
        compiler_params=_cparams(("parallel", "arbitrary"), 40),
    )(p16, p16, p16)


_HG_LEVELS = (32, 16, 8, 4, 2, 1)
HGRN_STEP = 2


def _hgrn_constants():
    C = HGRN_CHUNK
    n = np.arange(C)
    seg = np.zeros((8 * C, C), np.float32)
    pair = np.zeros((7, C, C), np.float32)
    for l, s in enumerate(_HG_LEVELS):
        blk = n // s
        odd = (blk % 2) == 1
        a = blk * s
        e = a + s - 1
        j = n[None, :]
        t_odd = (j >= a[:, None]) & (j <= n[:, None])
        t_even = (j > n[:, None]) & (j <= e[:, None])
        seg[l * C:(l + 1) * C] = np.where(odd[:, None], t_odd, t_even)
        pair[l] = odd[:, None] & (blk[None, :] == blk[:, None] - 1)
    seg[6 * C:7 * C] = n[None, :] <= n[:, None]
    seg[7 * C:8 * C] = n[None, :] > n[:, None]
    pair[6] = np.eye(C)
    return seg, pair


def _hgrn_kernel(layer, q_ref, f_ref, v_ref, g_ref, lbl_ref, gain_ref, seg_ref, pair_ref, o_ref, st_ref):
    C = HGRN_CHUNK

    @pl.when(pl.program_id(0) == 0)
    def _():
        st_ref[...] = jnp.zeros_like(st_ref)

    logits = lbl_ref[...]
    e = jnp.exp(logits - jnp.max(logits, axis=0, keepdims=True))
    p = e / jnp.sum(e, axis=0, keepdims=True)
    lb = jnp.zeros((1, GROUP_WIDTH), F32)
    for l in range(1, layer + 1):
        lb = lb + p[l:l + 1, :]

    f = lb + (1.0 - lb) * jax.nn.sigmoid(f_ref[...])
    log_f = jnp.log(jnp.maximum(f, MIN_FORGET))
    kk = 1.0 - f
    parts = _split3(log_f)
    zero = jnp.zeros((C, GROUP_WIDTH), BF16)
    rhs = jnp.concatenate(
        [jnp.concatenate([p[c * C:(c + 1) * C] for p in parts] + [zero], axis=0) for c in range(HGRN_STEP)],
        axis=1)
    ex = jnp.exp(_dot(seg_ref[...], rhs))

    for h in range(N_HEADS):
        hs = slice(h * HEAD_DIM, (h + 1) * HEAD_DIM)
        st = st_ref[h]
        for c in range(HGRN_STEP):
            rs = slice(c * C, (c + 1) * C)
            es = slice(c * GROUP_WIDTH + h * HEAD_DIM, c * GROUP_WIDTH + (h + 1) * HEAD_DIM)
            q = q_ref[rs, hs] * ATTN_SCALE
            k = kk[rs, hs]
            v = v_ref[rs, hs]
            s = pair_ref[6] * _dot_nt(q.astype(BF16), k.astype(BF16))
            for l in range(6):
                el = ex[l * C:(l + 1) * C, es]
                s = s + pair_ref[l] * _dot_nt((q * el).astype(BF16), (k * el).astype(BF16))
            e_cum = ex[6 * C:7 * C, es]
            e_rest = ex[7 * C:8 * C, es]
            o = _dot(s.astype(BF16), v) + _dot_nt((q * e_cum).astype(BF16), st.astype(BF16))
            st = st * e_cum[C - 1:C, :] + _dot_tn(v, (k * e_rest).astype(BF16))
            y = o * lax.rsqrt(jnp.mean(o * o, axis=-1, keepdims=True) + NORM_EPS) * gain_ref[:, hs]
            o_ref[rs, hs] = (y * _silu(g_ref[rs, hs])).astype(o_ref.dtype)
        st_ref[h] = st


def _hgrn2(p32, p16, lb_logits, gains, layer, S):
    C = HGRN_CHUNK
    rows = C * HGRN_STEP
    gw = GROUP_WIDTH // LANES
    seg, pair = _hgrn_constants()
    seg = np.concatenate([seg, seg, seg, np.zeros_like(seg)], axis=1)
    return pl.pallas_call(
        functools.partial(_hgrn_kernel, layer),
        grid=(S // rows,),
        in_specs=[pl.BlockSpec((rows, GROUP_WIDTH), lambda c: (c, COL_GQ // gw)),
                  pl.BlockSpec((rows, GROUP_WIDTH), lambda c: (c, COL_GF // gw)),
                  pl.BlockSpec((rows, GROUP_WIDTH), lambda c: (c, COL_GI // gw)),
                  pl.BlockSpec((rows, GROUP_WIDTH), lambda c: (c, COL_GG // gw)),
                  pl.BlockSpec((DEPTH, GROUP_WIDTH), lambda c: (0, 0)),
                  _layer_spec(layer, (1, GROUP_WIDTH)),
                  pl.BlockSpec((8 * C, 4 * C), lambda c: (0, 0)),
                  pl.BlockSpec((7, C, C), lambda c: (0, 0, 0))],
        out_specs=pl.BlockSpec((rows, GROUP_WIDTH), lambda c: (c, 0)),
        out_shape=jax.ShapeDtypeStruct((S, GROUP_WIDTH), BF16),
        scratch_shapes=[pltpu.VMEM((N_HEADS, HEAD_DIM, HEAD_DIM), F32)],
        compiler_params=_cparams(("arbitrary",), 32),
    )(p32, p32, p16, p32, lb_logits, gains,
      jnp.asarray(seg, BF16), jnp.asarray(pair, F32))


NSA_Q_SCALE = ATTN_SCALE * math.log2(math.e)
NSA_MAX_SLC = LANES


def _nsa_prep_kernel(q_ref, kc_ref, vc_ref, ks_ref, kw_ref, vs_ref, vw_ref, cos_ref, sin_ref,
                     qo_ref, kco_ref, vco_ref, kso_ref, kwo_ref, vso_ref, vwo_ref):
    tm = q_ref.shape[0]
    ones = jnp.ones((tm, HEAD_DIM), BF16)
    vso_ref[:, 0:HEAD_DIM] = vs_ref[...]
    vso_ref[:, HEAD_DIM:] = ones
    vwo_ref[:, 0:HEAD_DIM] = vw_ref[...]
    vwo_ref[:, HEAD_DIM:] = ones
    cos = cos_ref[...]
    sin = sin_ref[...]
    for h in range(N_HEADS):
        hs = slice(h * HEAD_DIM, (h + 1) * HEAD_DIM)
        qo_ref[:, hs] = (_rope(q_ref[:, hs], cos, sin) * NSA_Q_SCALE).astype(BF16)
    kco_ref[...] = _rope(kc_ref[...], cos, sin).astype(BF16)
    kwo_ref[...] = _rope(kw_ref[...], cos, sin).astype(BF16)
    vco_ref[...] = vc_ref[...]
    kso_ref[:, 0:HEAD_DIM] = _rope(ks_ref[...], cos, sin).astype(BF16)
    tok = pl.program_id(0) * tm + lax.broadcasted_iota(jnp.int32, (tm, NSA_MAX_SLC), 0)
    lane = lax.broadcasted_iota(jnp.int32, (tm, NSA_MAX_SLC), 1)
    kso_ref[:, HEAD_DIM:] = jnp.where(jnp.right_shift(tok, NSA_SLC_SHIFT) == lane, 1.0, 0.0).astype(BF16)


def _nsa_prep(p32, p16, cos, sin, S):
    tm = min(S, 512)
    gw = GROUP_WIDTH // LANES
    row = lambda col: pl.BlockSpec((tm, HEAD_DIM), lambda i: (i, col))
    out = pl.BlockSpec((tm, HEAD_DIM), lambda i: (i, 0))
    aug = pl.BlockSpec((tm, HEAD_DIM + NSA_MAX_SLC), lambda i: (i, 0))
    return pl.pallas_call(
        _nsa_prep_kernel,
        grid=(S // tm,),
        in_specs=[pl.BlockSpec((tm, GROUP_WIDTH), lambda i: (i, COL_NQ // gw)),
                  row(COL_KC), row(COL_VC), row(COL_KS), row(COL_KW), row(COL_VS), row(COL_VW), out, out],
        out_specs=[pl.BlockSpec((tm, GROUP_WIDTH), lambda i: (i, 0)), out, out, aug, out, aug, aug],
        out_shape=[jax.ShapeDtypeStruct((S, GROUP_WIDTH), BF16),
                   jax.ShapeDtypeStruct((S, HEAD_DIM), BF16),
                   jax.ShapeDtypeStruct((S, HEAD_DIM), BF16),
                   jax.ShapeDtypeStruct((S, HEAD_DIM + NSA_MAX_SLC), BF16),
                   jax.ShapeDtypeStruct((S, HEAD_DIM), BF16),
                   jax.ShapeDtypeStruct((S, 2 * HEAD_DIM), BF16),
                   jax.ShapeDtypeStruct((S, 2 * HEAD_DIM), BF16)],
        compiler_params=_cparams(("parallel",), 32),
    )(p32, p32, p16, p32, p32, p16, p16, cos, sin)


def _nsa_compress_kernel(kb_ref, vb_ref, wk_ref, wv_ref, pk_ref, pv_ref, ko_ref, vo_ref):
    nblk = kb_ref.shape[0]
    half = NSA_CMP_STRIDE * HEAD_DIM
    rowid = lax.broadcasted_iota(jnp.int32, (nblk, HEAD_DIM), 0)
    for b_ref, w_ref, p_ref, o_ref in ((kb_ref, wk_ref, pk_ref, ko_ref), (vb_ref, wv_ref, pv_ref, vo_ref)):
        w = w_ref[...]
        top = _dot(b_ref[...], w[:half])
        bot = _dot(b_ref[...], w[half:])
        ph, pm, plo = _split3(p_ref[...])
        const = _dot(ph, w) + _dot(pm, w) + _dot(plo, w)
        shifted = pltpu.roll(bot, nblk - 1, axis=0)
        out = top + shifted + const
        o_ref[...] = jnp.where(rowid < nblk - 1, out, 0.0).astype(BF16)


def _nsa_compress(kc16, vc16, wk16, wv16, pos_k, pos_v, layer, S):
    nblk = S // NSA_CMP_STRIDE
    width = NSA_CMP_STRIDE * HEAD_DIM
    kb = kc16.reshape(nblk, width)
    vb = vc16.reshape(nblk, width)
    full = lambda shape: pl.BlockSpec(shape, lambda i: (0,) * len(shape))
    return pl.pallas_call(
        _nsa_compress_kernel,
        grid=(1,),
        in_specs=[full((nblk, width)), full((nblk, width)),
                  _layer_spec(layer, (2 * width, HEAD_DIM)), _layer_spec(layer, (2 * width, HEAD_DIM)),
                  _layer_spec(layer, (1, 2 * width)), _layer_spec(layer, (1, 2 * width))],
        out_specs=[full((nblk, HEAD_DIM)), full((nblk, HEAD_DIM))],
        out_shape=[jax.ShapeDtypeStruct((nblk, HEAD_DIM), BF16)] * 2,
        compiler_params=_cparams(("arbitrary",), 32),
    )(kb, vb, wk16, wv16, pos_k, pos_v)


NSA_Q = 256
NSA_KT = 512
NSA_ROWS = N_HEADS * NSA_Q
NSA_OFF = -(2.0 ** 126)


def _lanes4(x):
    return jnp.maximum(jnp.maximum(x[:, 0:LANES], x[:, LANES:2 * LANES]),
                       jnp.maximum(x[:, 2 * LANES:3 * LANES], x[:, 3 * LANES:4 * LANES]))


def _nsa_attn_kernel(n_slc, n_cast, q_ref, g_ref, kcmp_ref, vcmp_ref, ovt_ref, ks_ref, vs_ref, kw_ref, vw_ref,
                     *refs):
    cast_in, o_ref, cast_out = refs[:n_cast], refs[n_cast], refs[n_cast + 1:2 * n_cast + 1]
    m_ref, acc_ref = refs[2 * n_cast + 1:]
    Q, KT, R = NSA_Q, NSA_KT, NSA_ROWS
    nsl = NSA_MAX_SLC
    bi = pl.program_id(0)
    for src, dst in zip(cast_in, cast_out):
        dst[...] = src[...].astype(BF16)
    q0 = bi * Q
    ncmp = kcmp_ref.shape[0]
    qs = jnp.concatenate([q_ref[:, h * HEAD_DIM:(h + 1) * HEAD_DIM] for h in range(N_HEADS)], axis=0)
    qpos_r = q0 + jnp.bitwise_and(lax.broadcasted_iota(jnp.int32, (R, 1), 0), Q - 1)

    last_vis = jnp.minimum(jnp.right_shift(qpos_r - (NSA_CMP_LEN - 1), 4), ncmp - 2)
    cmp_mask = lax.broadcasted_iota(jnp.int32, (R, ncmp), 1) <= last_vis
    s2 = jnp.where(cmp_mask, _dot_nt(qs, kcmp_ref[...]), MASK_VALUE)
    pc = jnp.exp2(s2 - jnp.max(s2, axis=-1, keepdims=True))
    inv = jnp.where(last_vis >= 0, 1.0 / jnp.sum(pc, axis=-1, keepdims=True), 0.0)
    pc = pc * inv
    o_cmp = _dot(pc.astype(BF16), vcmp_ref[...])

    W = NSA_WINDOW + Q
    ws = pl.multiple_of(jnp.maximum(q0 - NSA_WINDOW, 0), Q)
    dist = (qpos_r - ws) - lax.broadcasted_iota(jnp.int32, (R, W), 1)
    wmask = pltpu.bitcast(dist, jnp.uint32) < NSA_WINDOW
    s2 = jnp.where(wmask, _dot_nt(qs, kw_ref[pl.ds(ws, W), :]), MASK_VALUE)
    pw = jnp.exp2(s2 - jnp.max(s2, axis=-1, keepdims=True))
    ow = _dot(pw.astype(BF16), vw_ref[pl.ds(ws, W), :])
    o_win = ow[:, 0:HEAD_DIM] / ow[:, HEAD_DIM:]

    p_sum = pc[0:Q] + pc[Q:2 * Q] + pc[2 * Q:3 * Q] + pc[3 * Q:4 * Q]
    ph = p_sum.astype(BF16)
    plo = (p_sum - ph.astype(F32)).astype(BF16)
    imp_t = _dot_nt(ovt_ref[...], ph) + _dot_nt(ovt_ref[...], plo)
    blk = lax.broadcasted_iota(jnp.int32, (nsl, Q), 0)
    blkf = blk.astype(F32)
    qpos_l = q0 + lax.broadcasted_iota(jnp.int32, (nsl, Q), 1)
    cur = jnp.right_shift(qpos_l, NSA_SLC_SHIFT)
    forced = (blk == 0) | (blk == cur) | (blk == cur - 1)
    future = blk * NSA_SLC_LEN > qpos_l
    work = jnp.where(future, -1.0, jnp.where(forced, FORCED_SCORE, imp_t))
    work = jnp.where(blk < n_slc, work, -jnp.inf)
    sel = jnp.zeros((nsl, Q), F32)
    for _ in range(min(NSA_TOPK, n_slc)):
        mx = jnp.max(work, axis=0, keepdims=True)
        first = jnp.min(jnp.where(work == mx, blkf, float(nsl)), axis=0, keepdims=True)
        hit = blkf == first
        sel = jnp.where(hit, 1.0, sel)
        work = jnp.where(hit, -jnp.inf, work)
    bias_t = jnp.where(future, NSA_OFF, jnp.where(sel > 0.0, 0.0, NSA_OFF))
    bias = bias_t.T.astype(BF16)
    q_aug = jnp.concatenate([qs, jnp.concatenate([bias] * N_HEADS, axis=0)], axis=1)

    n_tiles = (q0 + Q + KT - 1) // KT

    def scores(t, width, causal):
        start = pl.multiple_of(t * KT, KT)
        s2 = _dot_nt(q_aug, ks_ref[pl.ds(start, width), :])
        if causal:
            kpos = start + lax.broadcasted_iota(jnp.int32, (R, width), 1)
            s2 = jnp.where(kpos <= qpos_r, s2, NSA_OFF)
        return start, s2

    def max_tile(t, width, causal):
        _, s2 = scores(t, width, causal)
        m = m_ref[...]
        for c in range(width // KT):
            m = jnp.maximum(m, _lanes4(s2[:, c * KT:(c + 1) * KT]))
        m_ref[...] = m

    def sum_tile(t, width, causal):
        start, s2 = scores(t, width, causal)
        pt = jnp.exp2(s2 - jnp.concatenate([m_ref[...]] * (width // LANES), axis=1))
        acc_ref[...] += _dot(pt.astype(BF16), vs_ref[pl.ds(start, width), :])

    def loop(fn):
        n_full = n_tiles - 1

        def body(u, carry):
            fn(2 * u, 2 * KT, False)
            return carry

        lax.fori_loop(0, n_full // 2, body, 0)

        @pl.when(n_full % 2 == 1)
        def _():
            fn(n_full - 1, KT, False)

        fn(n_tiles - 1, KT, True)

    m_ref[...] = jnp.full_like(m_ref, NSA_OFF)
    loop(max_tile)
    m_ref[...] = jnp.broadcast_to(jnp.max(m_ref[...], axis=-1, keepdims=True), m_ref.shape)
    acc_ref[...] = jnp.zeros_like(acc_ref)
    loop(sum_tile)
    o_slc = acc_ref[:, 0:HEAD_DIM] / acc_ref[:, HEAD_DIM:]

    gate = jax.nn.sigmoid(g_ref[...])
    for h in range(N_HEADS):
        rows = slice(h * Q, (h + 1) * Q)
        out = (gate[:, 3 * h:3 * h + 1] * o_cmp[rows] + gate[:, 3 * h + 1:3 * h + 2] * o_slc[rows]
               + gate[:, 3 * h + 2:3 * h + 3] * o_win[rows])
        o_ref[:, h * HEAD_DIM:(h + 1) * HEAD_DIM] = out.astype(o_ref.dtype)


def _nsa_overlap(S):
    n_cmp = (S - NSA_CMP_LEN) // NSA_CMP_STRIDE + 1
    n_slc = S // NSA_SLC_LEN
    nblk = S // NSA_CMP_STRIDE
    nsl = NSA_MAX_SLC
    assert n_slc <= nsl
    cmp_start = np.arange(n_cmp) * NSA_CMP_STRIDE
    slc_start = np.arange(n_slc) * NSA_SLC_LEN
    ov = (np.minimum(cmp_start[:, None] + NSA_CMP_LEN, slc_start[None, :] + NSA_SLC_LEN)
          - np.maximum(cmp_start[:, None], slc_start[None, :]))
    out = np.zeros((nblk, nsl), np.float32)
    out[:n_cmp, :n_slc] = np.clip(ov, 0, None) / NSA_CMP_LEN
    return out, n_slc


def _nsa_attention(qn16, p32, kcmp, vcmp, ks16, vs16, kw16, vw16, casts, layer, S):
    Q = NSA_Q
    steps = S // Q
    ov, n_slc = _nsa_overlap(S)
    nblk, nsl = ov.shape
    slabs = [(w.shape[1] // steps, w.shape[2]) for w in casts]
    full = lambda shape: pl.BlockSpec(shape, lambda i: (0,) * len(shape))
    outs = pl.pallas_call(
        functools.partial(_nsa_attn_kernel, n_slc, len(casts)),
        grid=(steps,),
        in_specs=[pl.BlockSpec((Q, GROUP_WIDTH), lambda i: (i, 0)),
                  pl.BlockSpec((Q, HEAD_DIM), lambda i: (i, COL_NG)),
                  full((nblk, HEAD_DIM)), full((nblk, HEAD_DIM)), full((nsl, nblk)),
                  full((S, HEAD_DIM + NSA_MAX_SLC)), full((S, 2 * HEAD_DIM)),
                  full((S, HEAD_DIM)), full((S, 2 * HEAD_DIM))]
        + [_layer_spec(layer, slab, lambda i: (i, 0)) for slab in slabs],
        out_specs=[pl.BlockSpec((Q, GROUP_WIDTH), lambda i: (i, 0))]
        + [pl.BlockSpec((None,) + slab, lambda i: (0, i, 0)) for slab in slabs],
        out_shape=[jax.ShapeDtypeStruct((S, GROUP_WIDTH), BF16)]
        + [jax.ShapeDtypeStruct((1,) + w.shape[1:], BF16) for w in casts],
        scratch_shapes=[pltpu.VMEM((NSA_ROWS, LANES), F32), pltpu.VMEM((NSA_ROWS, 2 * HEAD_DIM), F32)],
        compiler_params=_cparams(("arbitrary",), 56),
    )(qn16, p32, kcmp, vcmp, jnp.asarray(ov.T, BF16), ks16, vs16, kw16, vw16, *casts)
    return outs[0], outs[1:]


def _outproj_kernel(a0_ref, a1_ref, a2_ref, a3_ref, w_ref, x_ref, o_ref):
    acc = x_ref[...]
    for g, a_ref in enumerate((a0_ref, a1_ref, a2_ref, a3_ref)):
        acc = acc + _dot(a_ref[...], w_ref[g * GROUP_WIDTH:(g + 1) * GROUP_WIDTH, :])
    o_ref[...] = acc


def _outproj(parts, w16, x, tm):
    S, N = x.shape
    a_spec = pl.BlockSpec((tm, GROUP_WIDTH), lambda i: (i, 0))
    return pl.pallas_call(
        _outproj_kernel,
        grid=(S // tm,),
        in_specs=[a_spec, a_spec, a_spec, a_spec,
                  _layer_spec(0, (4 * GROUP_WIDTH, N)),
                  pl.BlockSpec((tm, N), lambda i: (i, 0))],
        out_specs=pl.BlockSpec((tm, N), lambda i: (i, 0)),
        out_shape=jax.ShapeDtypeStruct((S, N), F32),
        compiler_params=_cparams(("parallel",), 48),
    )(*parts, w16, x)


def _xattn_kernel(x_ref, g_ref, wq_ref, k_ref, v_ref, wo_ref, o_ref):
    x = x_ref[...]
    h = _rms(x, g_ref[...]).astype(BF16)
    q = _dot(h, wq_ref[...]).astype(BF16)
    outs = []
    for hd in range(N_HEADS):
        hs = slice(hd * HEAD_DIM, (hd + 1) * HEAD_DIM)
        s = _dot_nt(q[:, hs], k_ref[:, hs]) * ATTN_SCALE
        m = jnp.max(s, axis=-1, keepdims=True)
        p = jnp.exp(s - m)
        p = p / jnp.sum(p, axis=-1, keepdims=True)
        outs.append(_dot(p.astype(BF16), v_ref[:, hs]).astype(BF16))
    o = jnp.concatenate(outs, axis=-1)
    o_ref[...] = x + _dot(o, wo_ref[...])


def _xattn(x, gains, wq16, k16, v16, wo16, layer, tm):
    S, K = x.shape
    XW = wq16.shape[2]
    M = k16.shape[0]
    return pl.pallas_call(
        _xattn_kernel,
        grid=(S // tm,),
        in_specs=[pl.BlockSpec((tm, K), lambda i: (i, 0)),
                  _layer_spec(layer, (1, K)),
                  _layer_spec(0, (K, XW)),
                  pl.BlockSpec((M, XW), lambda i: (0, 0)),
                  pl.BlockSpec((M, XW), lambda i: (0, 0)),
                  _layer_spec(0, (XW, K))],
        out_specs=pl.BlockSpec((tm, K), lambda i: (i, 0)),
        out_shape=jax.ShapeDtypeStruct((S, K), F32),
        compiler_params=_cparams(("parallel",), 48),
    )(x, gains, wq16, k16, v16, wo16)


FFN_HALO = 16
FFN_SUB = 512


def _ffn_kernel(final, x_ref, xp_ref, g_ref, wg_ref, wv_ref, cwg_ref, cwv_ref, cbg_ref, cbv_ref, wd_ref,
                fg_ref, o_ref, h_ref):
    i = pl.program_id(0)
    f = pl.program_id(1)
    tf = wg_ref.shape[1]

    @pl.when(f == 0)
    def _():
        hp = _rms(xp_ref[...], g_ref[...])
        h_ref[0:FFN_HALO, :] = jnp.where(i > 0, hp, 0.0).astype(BF16)
        h_ref[FFN_HALO:, :] = _rms(x_ref[...], g_ref[...]).astype(BF16)
        o_ref[...] = x_ref[...]

    h = h_ref[...]

    def conv(w_ref, cw_ref, cb_ref, cols):
        u = _dot(h, w_ref[:, cols])
        u1 = pltpu.roll(u, 1, axis=0)
        u2 = pltpu.roll(u, 2, axis=0)
        c = cb_ref[:, cols] + cw_ref[2:3, cols] * u + cw_ref[1:2, cols] * u1 + cw_ref[0:1, cols] * u2
        return c[FFN_HALO:, :]

    acts = []
    for s in range(tf // FFN_SUB):
        cols = slice(s * FFN_SUB, (s + 1) * FFN_SUB)
        gate = conv(wg_ref, cwg_ref, cbg_ref, cols)
        val = conv(wv_ref, cwv_ref, cbv_ref, cols)
        acts.append((_silu(gate) * val).astype(BF16))
    o_ref[...] += _dot(jnp.concatenate(acts, axis=1), wd_ref[...])

    if final:
        @pl.when(f == pl.num_programs(1) - 1)
        def _():
            o_ref[...] = _rms(o_ref[...], fg_ref[...])


def _ffn(x, gains, w_up16, conv_w, conv_b, w_down16, final_gain, layer, tm, tf):
    S, K = x.shape
    F = w_down16.shape[1]
    nf = F // tf
    hb = tm // FFN_HALO
    return pl.pallas_call(
        functools.partial(_ffn_kernel, layer == DEPTH - 1),
        grid=(S // tm, nf),
        in_specs=[pl.BlockSpec((tm, K), lambda i, f: (i, 0), pipeline_mode=pl.Buffered(1)),
                  pl.BlockSpec((FFN_HALO, K), lambda i, f: (jnp.maximum(i * hb - 1, 0), 0)),
                  _layer_spec(layer, (1, K)),
                  _layer_spec(0, (K, tf), lambda i, f: (0, f)),
                  _layer_spec(0, (K, tf), lambda i, f: (0, nf + f)),
                  _layer_spec(layer, (3, tf), lambda i, f: (0, f)),
                  _layer_spec(layer, (3, tf), lambda i, f: (0, nf + f)),
                  _layer_spec(layer, (1, tf), lambda i, f: (0, f)),
                  _layer_spec(layer, (1, tf), lambda i, f: (0, nf + f)),
                  _layer_spec(0, (tf, K), lambda i, f: (f, 0)),
                  pl.BlockSpec((1, K), lambda i, f: (0, 0))],
        out_specs=pl.BlockSpec((tm, K), lambda i, f: (i, 0)),
        out_shape=jax.ShapeDtypeStruct((S, K), F32),
        scratch_shapes=[pltpu.VMEM((tm + FFN_HALO, K), BF16)],
        compiler_params=_cparams(("parallel", "arbitrary"), 56),
    )(x, x, gains, w_up16, w_up16, conv_w, conv_w, conv_b, conv_b, w_down16, final_gain)


def kernel(x, mem, positions, mix_norm, w_in, ret_norm, hgrn_lb_logits, hgrn_norm, nsa_pos_k, nsa_pos_v,
           nsa_w_ck, nsa_w_cv, w_out, xattn_norm, mem_norm, xattn_wq, xattn_wk, xattn_wv, xattn_wo,
           ffn_norm, ffn_w_up, ffn_conv_w, ffn_conv_b, ffn_w_down, final_norm):
    B, S, _ = x.shape
    assert B == 1 and S % 1024 == 0
    L = DEPTH
    xs = x.reshape(S, D_MODEL)
    mem2 = mem.reshape(N_MEM, D_MODEL)
    row = lambda g: g.reshape(L, 1, g.shape[-1])
    w_in16 = jnp.pad(w_in.astype(BF16), ((0, 0), (0, 0), (0, IN_COLS_PAD - IN_COLS)))
    wck16 = nsa_w_ck.reshape(L, NSA_CMP_LEN * HEAD_DIM, HEAD_DIM).astype(BF16)
    wcv16 = nsa_w_cv.reshape(L, NSA_CMP_LEN * HEAD_DIM, HEAD_DIM).astype(BF16)
    pos_k = nsa_pos_k.reshape(L, 1, NSA_CMP_LEN * HEAD_DIM)
    pos_v = nsa_pos_v.reshape(L, 1, NSA_CMP_LEN * HEAD_DIM)
    mix_g, ret_g, hgrn_g, xattn_g, mem_g, ffn_g = (
        row(g) for g in (mix_norm, ret_norm, hgrn_norm, xattn_norm, mem_norm, ffn_norm))
    conv_b = row(ffn_conv_b)
    final_g = final_norm.reshape(1, D_MODEL)

    cos, sin = _rope_tables(positions, S)
    for layer in range(L):
        p32, p16 = _inproj(xs, mix_g, w_in16, layer, PROJ_TM, PROJ_TN)
        o_ret = _retention(p32, p16, cos, sin, ret_g, layer, S)
        o_sb = _stick_breaking(p16, S)
        o_hg = _hgrn2(p32, p16, hgrn_lb_logits, hgrn_g, layer, S)
        qn16, kc16, vc16, ks16, kw16, vs16, vw16 = _nsa_prep(p32, p16, cos, sin, S)
        kcmp, vcmp = _nsa_compress(kc16, vc16, wck16, wcv16, pos_k, pos_v, layer, S)
        o_nsa, (w_out16, wq16, wk16, wv16, wo16, w_up16, w_down16) = _nsa_attention(
            qn16, p32, kcmp, vcmp, ks16, vs16, kw16, vw16,
            (w_out, xattn_wq, xattn_wk, xattn_wv, xattn_wo, ffn_w_up, ffn_w_down), layer, S)
        xs = _outproj((o_ret, o_sb, o_hg, o_nsa), w_out16, xs, ROW_TM)
        k16 = _normproj(mem2, mem_g, wk16, layer, BF16)
        v16 = _normproj(mem2, mem_g, wv16, layer, BF16)
        xs = _xattn(xs, xattn_g, wq16, k16, v16, wo16, layer, ROW_TM)
        xs = _ffn(xs, ffn_g, w_up16, ffn_conv_w, conv_b, w_down16, final_g, layer, FFN_TM, FFN_TF)
    return xs.reshape(B, S, D_MODEL)
```

```python
import functools
import math

import numpy as np
import jax
import jax.numpy as jnp
from jax import lax
from jax.experimental import pallas as pl
from jax.experimental.pallas import tpu as pltpu

F32 = jnp.float32
BF16 = jnp.bfloat16

D_MODEL = 2048
DEPTH = 4
N_HEADS = 4
HEAD_DIM = 128
GROUP_WIDTH = N_HEADS * HEAD_DIM
ROPE_THETA = 10000.0
RET_CHUNK = 128
HGRN_CHUNK = 64
NSA_CMP_LEN = 32
NSA_CMP_STRIDE = 16
NSA_SLC_LEN = 64
NSA_SLC_SHIFT = 6
NSA_TOPK = 16
NSA_WINDOW = 512
N_MEM = 256
D_FF = 5632
NORM_EPS = 1e-6
MASK_VALUE = -1e30
FORCED_SCORE = 1e4
MIN_FORGET = 1e-6
IN_COLS = 12 * GROUP_WIDTH + 6 * HEAD_DIM + 3 * N_HEADS
ATTN_SCALE = HEAD_DIM ** -0.5
LANES = 128


IN_COLS_PAD = 7168

COL_RQ, COL_RK, COL_RV, COL_RG = 0, 4, 8, 12
COL_SQ, COL_SK, COL_SV = 16, 20, 24
COL_GQ, COL_GF, COL_GI, COL_GG = 28, 32, 36, 40
COL_NQ = 44
COL_KC, COL_VC, COL_KS, COL_VS, COL_KW, COL_VW, COL_NG = 48, 49, 50, 51, 52, 53, 54

V7X_VMEM_BYTES = 64 * 1024 * 1024

PROJ_TM, PROJ_TN = 1024, 1024
ROW_TM = 512
FFN_TM, FFN_TF = 1024, 512


def _cparams(semantics, vmem_mb):
    return pltpu.CompilerParams(dimension_semantics=semantics,
                                vmem_limit_bytes=min(vmem_mb * 1024 * 1024, V7X_VMEM_BYTES - (8 << 20)))


def _layer_spec(layer, tail_shape, tail_index=None):
    if tail_index is None:
        tail_index = lambda *ids: (0,) * len(tail_shape)
    return pl.BlockSpec((None,) + tuple(tail_shape), lambda *ids: (layer,) + tuple(tail_index(*ids)))


def _dot(a, b):
    return jnp.dot(a, b, preferred_element_type=F32)


def _dot_nt(a, b):
    return lax.dot_general(a, b, (((1,), (1,)), ((), ())), preferred_element_type=F32)


def _dot_tn(a, b):
    return lax.dot_general(a, b, (((0,), (0,)), ((), ())), preferred_element_type=F32)


def _split3(x):
    h1 = x.astype(BF16)
    r1 = x - h1.astype(F32)
    h2 = r1.astype(BF16)
    r2 = r1 - h2.astype(F32)
    return h1, h2, r2.astype(BF16)


def _rms(x, gain):
    ms = jnp.mean(x * x, axis=-1, keepdims=True)
    return x * lax.rsqrt(ms + NORM_EPS) * gain


def _rope(x, cos, sin_signed):
    return x * cos + pltpu.roll(x, HEAD_DIM // 2, axis=1) * sin_signed


def _silu(x):
    return x * jax.nn.sigmoid(x)


def _rope_table_kernel(pos_ref, inv_ref, cos_ref, sin_ref):
    ang = pos_ref[...].astype(F32) * inv_ref[...]
    lane = lax.broadcasted_iota(jnp.int32, ang.shape, 1)
    cos_ref[...] = jnp.cos(ang)
    s = jnp.sin(ang)
    sin_ref[...] = jnp.where(lane < HEAD_DIM // 2, -s, s)


def _rope_tables(positions, S):
    half = HEAD_DIM // 2
    inv = ROPE_THETA ** (-jnp.arange(half, dtype=F32) / half)
    inv = jnp.concatenate([inv, inv]).reshape(1, HEAD_DIM)
    pos = positions.reshape(S, 1)
    tm = min(S, 1024)
    return pl.pallas_call(
        _rope_table_kernel,
        grid=(S // tm,),
        in_specs=[pl.BlockSpec((tm, 1), lambda i: (i, 0)),
                  pl.BlockSpec((1, HEAD_DIM), lambda i: (0, 0))],
        out_specs=[pl.BlockSpec((tm, HEAD_DIM), lambda i: (i, 0)),
                   pl.BlockSpec((tm, HEAD_DIM), lambda i: (i, 0))],
        out_shape=[jax.ShapeDtypeStruct((S, HEAD_DIM), F32)] * 2,
        compiler_params=_cparams(("arbitrary",), 32),
    )(pos, inv)


def _inproj_kernel(x_ref, g_ref, w_ref, o32_ref, o16_ref, h_ref):
    @pl.when(pl.program_id(1) == 0)
    def _():
        h_ref[...] = _rms(x_ref[...], g_ref[...]).astype(BF16)

    y = _dot(h_ref[...], w_ref[...])
    o32_ref[...] = y
    o16_ref[...] = y.astype(BF16)


def _inproj(x, gains, w16, layer, tm, tn):
    S, K = x.shape
    N = w16.shape[2]
    return pl.pallas_call(
        _inproj_kernel,
        grid=(S // tm, N // tn),
        in_specs=[pl.BlockSpec((tm, K), lambda i, j: (i, 0)),
                  _layer_spec(layer, (1, K)),
                  _layer_spec(layer, (K, tn), lambda i, j: (0, j))],
        out_specs=[pl.BlockSpec((tm, tn), lambda i, j: (i, j)),
                   pl.BlockSpec((tm, tn), lambda i, j: (i, j))],
        out_shape=[jax.ShapeDtypeStruct((S, N), F32), jax.ShapeDtypeStruct((S, N), BF16)],
        scratch_shapes=[pltpu.VMEM((tm, K), BF16)],
        compiler_params=_cparams(("parallel", "arbitrary"), 48),
    )(x, gains, w16)


def _normproj_kernel(x_ref, g_ref, w_ref, o_ref):
    h = _rms(x_ref[...], g_ref[...]).astype(BF16)
    o_ref[...] = _dot(h, w_ref[...]).astype(o_ref.dtype)


def _normproj(x, gains, w16, layer, out_dtype):
    M, K = x.shape
    N = w16.shape[2]
    return pl.pallas_call(
        _normproj_kernel,
        grid=(1,),
        in_specs=[pl.BlockSpec((M, K), lambda i: (0, 0)),
                  _layer_spec(layer, (1, K)),
                  _layer_spec(0, (K, N))],
        out_specs=pl.BlockSpec((M, N), lambda i: (0, 0)),
        out_shape=jax.ShapeDtypeStruct((M, N), out_dtype),
        compiler_params=_cparams(("arbitrary",), 32),
    )(x, gains, w16)


_RET_LOG_GAMMA = [math.log1p(-(2.0 ** (-5.0 - h))) for h in range(N_HEADS)]
RET_STEP = 4


def _ret_kernel(q_ref, k_ref, v_ref, g_ref, cos_ref, sin_ref, gain_ref, o_ref, st_ref):
    C = RET_CHUNK

    @pl.when(pl.program_id(0) == 0)
    def _():
        st_ref[...] = jnp.zeros_like(st_ref)

    row = lax.broadcasted_iota(jnp.int32, (C, C), 0)
    col = lax.broadcasted_iota(jnp.int32, (C, C), 1)
    rel = (row - col).astype(F32)
    rowf = row.astype(F32)
    for h in range(N_HEADS):
        hs = slice(h * HEAD_DIM, (h + 1) * HEAD_DIM)
        lg = _RET_LOG_GAMMA[h]
        decay = jnp.where(rel >= 0, jnp.exp(lg * jnp.maximum(rel, 0.0)), 0.0)
        q_w = jnp.exp(lg * (rowf + 1.0))
        k_w = jnp.exp(lg * (C - 1.0 - rowf))
        st = st_ref[h]
        for c in range(RET_STEP):
            rs = slice(c * C, (c + 1) * C)
            cos = cos_ref[rs, :]
            sin = sin_ref[rs, :]
            q = _rope(q_ref[rs, hs], cos, sin)
            k = _rope(k_ref[rs, hs], cos, sin) * ATTN_SCALE
            v = v_ref[rs, hs]
            s = _dot_nt(q.astype(BF16), k.astype(BF16)) * decay
            o = _dot(s.astype(BF16), v) + _dot_nt((q * q_w).astype(BF16), st.astype(BF16))
            st = st * math.exp(lg * C) + _dot_tn(v, (k * k_w).astype(BF16))
            oc = o - jnp.mean(o, axis=-1, keepdims=True)
            y = oc * lax.rsqrt(jnp.mean(oc * oc, axis=-1, keepdims=True) + NORM_EPS) * gain_ref[:, hs]
            o_ref[rs, hs] = (y * _silu(g_ref[rs, hs])).astype(o_ref.dtype)
        st_ref[h] = st


def _retention(p32, p16, cos, sin, gains, layer, S):
    rows = RET_CHUNK * RET_STEP
    gw = GROUP_WIDTH // LANES
    return pl.pallas_call(
        _ret_kernel,
        grid=(S // rows,),
        in_specs=[pl.BlockSpec((rows, GROUP_WIDTH), lambda c: (c, COL_RQ // gw)),
                  pl.BlockSpec((rows, GROUP_WIDTH), lambda c: (c, COL_RK // gw)),
                  pl.BlockSpec((rows, GROUP_WIDTH), lambda c: (c, COL_RV // gw)),
                  pl.BlockSpec((rows, GROUP_WIDTH), lambda c: (c, COL_RG // gw)),
                  pl.BlockSpec((rows, HEAD_DIM), lambda c: (c, 0)),
                  pl.BlockSpec((rows, HEAD_DIM), lambda c: (c, 0)),
                  _layer_spec(layer, (1, GROUP_WIDTH))],
        out_specs=pl.BlockSpec((rows, GROUP_WIDTH), lambda c: (c, 0)),
        out_shape=jax.ShapeDtypeStruct((S, GROUP_WIDTH), BF16),
        scratch_shapes=[pltpu.VMEM((N_HEADS, HEAD_DIM, HEAD_DIM), F32)],
        compiler_params=_cparams(("arbitrary",), 32),
    )(p32, p32, p16, p32, cos, sin, gains)


SB_TILE = 256
SB_HEADS = 4
SB_DEAD_LOG = -104.0


def _sb_tile(q, k_ref, v_ref, hs, j, run, acc, upper, diag, on=None):
    T = SB_TILE
    start = pl.multiple_of(j * T, T)
    k = k_ref[pl.ds(start, T), hs]
    v = v_ref[pl.ds(start, T), hs]
    z = _dot_nt(q, k) * ATTN_SCALE
    sp = jnp.maximum(z, 0.0) + jnp.log(1.0 + jnp.exp(-jnp.abs(z)))
    log_1m = -sp
    strict = on
    if diag:
        row = lax.broadcasted_iota(jnp.int32, (T, T), 0)
        col = lax.broadcasted_iota(jnp.int32, (T, T), 1)
        strict = col < row
    if strict is not None:
        log_1m = jnp.where(strict, log_1m, 0.0)
    hi = log_1m.astype(BF16)
    lo = (log_1m - hi.astype(F32)).astype(BF16)
    both = _dot(jnp.concatenate([hi, lo], axis=0), upper)
    between = both[:T] + both[T:] + run
    w = jnp.exp((z - sp) + between)
    if strict is not None:
        w = jnp.where(strict, w, 0.0)
    acc = acc + _dot(w.astype(BF16), v)
    run = between[:, 0:1] + log_1m[:, 0:1]
    return run, acc


def _sb_kernel(q_ref, k_ref, v_ref, o_ref):
    T = SB_TILE
    i = pl.program_id(1)
    r = lax.broadcasted_iota(jnp.int32, (T, T), 0)
    c = lax.broadcasted_iota(jnp.int32, (T, T), 1)
    upper = jnp.where(r > c, 1.0, 0.0).astype(BF16)

    def alive(run):
        return (jnp.max(run) > SB_DEAD_LOG).astype(jnp.int32)

    heads = []
    for h in range(SB_HEADS):
        hs = slice(h * HEAD_DIM, (h + 1) * HEAD_DIM)
        q = q_ref[:, hs]
        run = jnp.zeros((T, 1), F32)
        acc = jnp.zeros((T, HEAD_DIM), F32)
        run, acc = _sb_tile(q, k_ref, v_ref, hs, i, run, acc, upper, True)
        run, acc = _sb_tile(q, k_ref, v_ref, hs, jnp.maximum(i - 1, 0), run, acc, upper, False, on=i > 0)
        heads.append((hs, q, run, acc))

    for hs, q, run, acc in heads:
        def cond(carry):
            step, live, _, _ = carry
            return jnp.logical_and(step < i, live > 0)

        def body(carry, hs=hs, q=q):
            step, _, run, acc = carry
            run, acc = _sb_tile(q, k_ref, v_ref, hs, i - 1 - step, run, acc, upper, False)
            return step + 1, alive(run), run, acc

        _, _, _, acc = lax.while_loop(cond, body, (jnp.int32(1), alive(run), run, acc))
        o_ref[:, hs] = acc.astype(o_ref.dtype)


def _stick_breaking(p16, S):
    T = SB_TILE
    W = SB_HEADS * HEAD_DIM
    return pl.pallas_call(
        _sb_kernel,
        grid=(N_HEADS // SB_HEADS, S // T),
        in_specs=[pl.BlockSpec((T, W), lambda h, i: (i, COL_SQ // SB_HEADS + h)),
                  pl.BlockSpec((S, W), lambda h, i: (0, COL_SK // SB_HEADS + h)),
                  pl.BlockSpec((S, W), lambda h, i: (0, COL_SV // SB_HEADS + h))],
        out_specs=pl.BlockSpec((T, W), lambda h, i: (i, h)),
        out_shape=jax.ShapeDtypeStruct((S, GROUP_WIDTH), BF16),
        compiler_params=_cparams(("parallel", "arbitrary"), 40),
    )(p16, p16, p16)


_HG_LEVELS = (32, 16, 8, 4, 2, 1)
HGRN_STEP = 4


def _hgrn_constants():
    C = HGRN_CHUNK
    n = np.arange(C)
    seg = np.zeros((8 * C, C), np.float32)
    pair = np.zeros((7, C, C), np.float32)
    for l, s in enumerate(_HG_LEVELS):
        blk = n // s
        odd = (blk % 2) == 1
        a = blk * s
        e = a + s - 1
        j = n[None, :]
        t_odd = (j >= a[:, None]) & (j <= n[:, None])
        t_even = (j > n[:, None]) & (j <= e[:, None])
        seg[l * C:(l + 1) * C] = np.where(odd[:, None], t_odd, t_even)
        pair[l] = odd[:, None] & (blk[None, :] == blk[:, None] - 1)
    seg[6 * C:7 * C] = n[None, :] <= n[:, None]
    seg[7 * C:8 * C] = n[None, :] > n[:, None]
    pair[6] = np.eye(C)
    return seg, pair


def _hgrn_kernel(layer, q_ref, f_ref, v_ref, g_ref, lbl_ref, gain_ref, seg_ref, pair_ref, o_ref, st_ref):
    C = HGRN_CHUNK

    @pl.when(pl.program_id(0) == 0)
    def _():
        st_ref[...] = jnp.zeros_like(st_ref)

    logits = lbl_ref[...]
    e = jnp.exp(logits - jnp.max(logits, axis=0, keepdims=True))
    p = e / jnp.sum(e, axis=0, keepdims=True)
    lb = jnp.zeros((1, GROUP_WIDTH), F32)
    for l in range(1, layer + 1):
        lb = lb + p[l:l + 1, :]

    f = lb + (1.0 - lb) * jax.nn.sigmoid(f_ref[...])
    log_f = jnp.log(jnp.maximum(f, MIN_FORGET))
    kk = 1.0 - f
    parts = _split3(log_f)
    zero = jnp.zeros((C, GROUP_WIDTH), BF16)
    rhs = jnp.concatenate(
        [jnp.concatenate([p[c * C:(c + 1) * C] for p in parts] + [zero], axis=0) for c in range(HGRN_STEP)],
        axis=1)
    ex = jnp.exp(_dot(seg_ref[...], rhs))

    for h in range(N_HEADS):
        hs = slice(h * HEAD_DIM, (h + 1) * HEAD_DIM)
        st = st_ref[h]
        for c in range(HGRN_STEP):
            rs = slice(c * C, (c + 1) * C)
            es = slice(c * GROUP_WIDTH + h * HEAD_DIM, c * GROUP_WIDTH + (h + 1) * HEAD_DIM)
            q = q_ref[rs, hs] * ATTN_SCALE
            k = kk[rs, hs]
            v = v_ref[rs, hs]
            s = pair_ref[6] * _dot_nt(q.astype(BF16), k.astype(BF16))
            for l in range(6):
                el = ex[l * C:(l + 1) * C, es]
                s = s + pair_ref[l] * _dot_nt((q * el).astype(BF16), (k * el).astype(BF16))
            e_cum = ex[6 * C:7 * C, es]
            e_rest = ex[7 * C:8 * C, es]
            o = _dot(s.astype(BF16), v) + _dot_nt((q * e_cum).astype(BF16), st.astype(BF16))
            st = st * e_cum[C - 1:C, :] + _dot_tn(v, (k * e_rest).astype(BF16))
            y = o * lax.rsqrt(jnp.mean(o * o, axis=-1, keepdims=True) + NORM_EPS) * gain_ref[:, hs]
            o_ref[rs, hs] = (y * _silu(g_ref[rs, hs])).astype(o_ref.dtype)
        st_ref[h] = st


def _hgrn2(p32, p16, lb_logits, gains, layer, S):
    C = HGRN_CHUNK
    rows = C * HGRN_STEP
    gw = GROUP_WIDTH // LANES
    seg, pair = _hgrn_constants()
    seg = np.concatenate([seg, seg, seg, np.zeros_like(seg)], axis=1)
    return pl.pallas_call(
        functools.partial(_hgrn_kernel, layer),
        grid=(S // rows,),
        in_specs=[pl.BlockSpec((rows, GROUP_WIDTH), lambda c: (c, COL_GQ // gw)),
                  pl.BlockSpec((rows, GROUP_WIDTH), lambda c: (c, COL_GF // gw)),
                  pl.BlockSpec((rows, GROUP_WIDTH), lambda c: (c, COL_GI // gw)),
                  pl.BlockSpec((rows, GROUP_WIDTH), lambda c: (c, COL_GG // gw)),
                  pl.BlockSpec((DEPTH, GROUP_WIDTH), lambda c: (0, 0)),
                  _layer_spec(layer, (1, GROUP_WIDTH)),
                  pl.BlockSpec((8 * C, 4 * C), lambda c: (0, 0)),
                  pl.BlockSpec((7, C, C), lambda c: (0, 0, 0))],
        out_specs=pl.BlockSpec((rows, GROUP_WIDTH), lambda c: (c, 0)),
        out_shape=jax.ShapeDtypeStruct((S, GROUP_WIDTH), BF16),
        scratch_shapes=[pltpu.VMEM((N_HEADS, HEAD_DIM, HEAD_DIM), F32)],
        compiler_params=_cparams(("arbitrary",), 32),
    )(p32, p32, p16, p32, lb_logits, gains,
      jnp.asarray(seg, BF16), jnp.asarray(pair, F32))


NSA_Q_SCALE = ATTN_SCALE * math.log2(math.e)
NSA_MAX_SLC = LANES


def _nsa_prep_kernel(q_ref, kc_ref, vc_ref, ks_ref, kw_ref, vs_ref, vw_ref, cos_ref, sin_ref,
                     qo_ref, kco_ref, vco_ref, kso_ref, kwo_ref, vso_ref, vwo_ref):
    tm = q_ref.shape[0]
    ones = jnp.ones((tm, HEAD_DIM), BF16)
    vso_ref[:, 0:HEAD_DIM] = vs_ref[...]
    vso_ref[:, HEAD_DIM:] = ones
    vwo_ref[:, 0:HEAD_DIM] = vw_ref[...]
    vwo_ref[:, HEAD_DIM:] = ones
    cos = cos_ref[...]
    sin = sin_ref[...]
    for h in range(N_HEADS):
        hs = slice(h * HEAD_DIM, (h + 1) * HEAD_DIM)
        qo_ref[:, hs] = (_rope(q_ref[:, hs], cos, sin) * NSA_Q_SCALE).astype(BF16)
    kco_ref[...] = _rope(kc_ref[...], cos, sin).astype(BF16)
    kwo_ref[...] = _rope(kw_ref[...], cos, sin).astype(BF16)
    vco_ref[...] = vc_ref[...]
    kso_ref[:, 0:HEAD_DIM] = _rope(ks_ref[...], cos, sin).astype(BF16)
    tok = pl.program_id(0) * tm + lax.broadcasted_iota(jnp.int32, (tm, NSA_MAX_SLC), 0)
    lane = lax.broadcasted_iota(jnp.int32, (tm, NSA_MAX_SLC), 1)
    kso_ref[:, HEAD_DIM:] = jnp.where(jnp.right_shift(tok, NSA_SLC_SHIFT) == lane, 1.0, 0.0).astype(BF16)


def _nsa_prep(p32, p16, cos, sin, S):
    tm = min(S, 512)
    gw = GROUP_WIDTH // LANES
    row = lambda col: pl.BlockSpec((tm, HEAD_DIM), lambda i: (i, col))
    out = pl.BlockSpec((tm, HEAD_DIM), lambda i: (i, 0))
    aug = pl.BlockSpec((tm, HEAD_DIM + NSA_MAX_SLC), lambda i: (i, 0))
    return pl.pallas_call(
        _nsa_prep_kernel,
        grid=(S // tm,),
        in_specs=[pl.BlockSpec((tm, GROUP_WIDTH), lambda i: (i, COL_NQ // gw)),
                  row(COL_KC), row(COL_VC), row(COL_KS), row(COL_KW), row(COL_VS), row(COL_VW), out, out],
        out_specs=[pl.BlockSpec((tm, GROUP_WIDTH), lambda i: (i, 0)), out, out, aug, out, aug, aug],
        out_shape=[jax.ShapeDtypeStruct((S, GROUP_WIDTH), BF16),
                   jax.ShapeDtypeStruct((S, HEAD_DIM), BF16),
                   jax.ShapeDtypeStruct((S, HEAD_DIM), BF16),
                   jax.ShapeDtypeStruct((S, HEAD_DIM + NSA_MAX_SLC), BF16),
                   jax.ShapeDtypeStruct((S, HEAD_DIM), BF16),
                   jax.ShapeDtypeStruct((S, 2 * HEAD_DIM), BF16),
                   jax.ShapeDtypeStruct((S, 2 * HEAD_DIM), BF16)],
        compiler_params=_cparams(("parallel",), 32),
    )(p32, p32, p16, p32, p32, p16, p16, cos, sin)


def _nsa_compress_kernel(kb_ref, vb_ref, wk_ref, wv_ref, pk_ref, pv_ref, ko_ref, vo_ref):
    nblk = kb_ref.shape[0]
    half = NSA_CMP_STRIDE * HEAD_DIM
    rowid = lax.broadcasted_iota(jnp.int32, (nblk, HEAD_DIM), 0)
    for b_ref, w_ref, p_ref, o_ref in ((kb_ref, wk_ref, pk_ref, ko_ref), (vb_ref, wv_ref, pv_ref, vo_ref)):
        w = w_ref[...]
        top = _dot(b_ref[...], w[:half])
        bot = _dot(b_ref[...], w[half:])
        ph, pm, plo = _split3(p_ref[...])
        const = _dot(ph, w) + _dot(pm, w) + _dot(plo, w)
        shifted = pltpu.roll(bot, nblk - 1, axis=0)
        out = top + shifted + const
        o_ref[...] = jnp.where(rowid < nblk - 1, out, 0.0).astype(BF16)


def _nsa_compress(kc16, vc16, wk16, wv16, pos_k, pos_v, layer, S):
    nblk = S // NSA_CMP_STRIDE
    width = NSA_CMP_STRIDE * HEAD_DIM
    kb = kc16.reshape(nblk, width)
    vb = vc16.reshape(nblk, width)
    full = lambda shape: pl.BlockSpec(shape, lambda i: (0,) * len(shape))
    return pl.pallas_call(
        _nsa_compress_kernel,
        grid=(1,),
        in_specs=[full((nblk, width)), full((nblk, width)),
                  _layer_spec(layer, (2 * width, HEAD_DIM)), _layer_spec(layer, (2 * width, HEAD_DIM)),
                  _layer_spec(layer, (1, 2 * width)), _layer_spec(layer, (1, 2 * width))],
        out_specs=[full((nblk, HEAD_DIM)), full((nblk, HEAD_DIM))],
        out_shape=[jax.ShapeDtypeStruct((nblk, HEAD_DIM), BF16)] * 2,
        compiler_params=_cparams(("arbitrary",), 32),
    )(kb, vb, wk16, wv16, pos_k, pos_v)


NSA_Q = 256
NSA_KT = 512
NSA_ROWS = N_HEADS * NSA_Q
NSA_OFF = -(2.0 ** 126)


def _lanes4(x):
    return jnp.maximum(jnp.maximum(x[:, 0:LANES], x[:, LANES:2 * LANES]),
                       jnp.maximum(x[:, 2 * LANES:3 * LANES], x[:, 3 * LANES:4 * LANES]))


def _nsa_attn_kernel(n_slc, n_cast, q_ref, g_ref, kcmp_ref, vcmp_ref, ovt_ref, ks_ref, vs_ref, kw_ref, vw_ref,
                     *refs):
    cast_in, o_ref, cast_out = refs[:n_cast], refs[n_cast], refs[n_cast + 1:2 * n_cast + 1]
    m_ref, acc_ref = refs[2 * n_cast + 1:]
    Q, KT, R = NSA_Q, NSA_KT, NSA_ROWS
    nsl = NSA_MAX_SLC
    bi = pl.program_id(0)
    for src, dst in zip(cast_in, cast_out):
        dst[...] = src[...].astype(BF16)
    q0 = bi * Q
    ncmp = kcmp_ref.shape[0]
    qs = jnp.concatenate([q_ref[:, h * HEAD_DIM:(h + 1) * HEAD_DIM] for h in range(N_HEADS)], axis=0)
    qpos_r = q0 + jnp.bitwise_and(lax.broadcasted_iota(jnp.int32, (R, 1), 0), Q - 1)

    last_vis = jnp.minimum(jnp.right_shift(qpos_r - (NSA_CMP_LEN - 1), 4), ncmp - 2)
    cmp_mask = lax.broadcasted_iota(jnp.int32, (R, ncmp), 1) <= last_vis
    s2 = jnp.where(cmp_mask, _dot_nt(qs, kcmp_ref[...]), MASK_VALUE)
    pc = jnp.exp2(s2 - jnp.max(s2, axis=-1, keepdims=True))
    inv = jnp.where(last_vis >= 0, 1.0 / jnp.sum(pc, axis=-1, keepdims=True), 0.0)
    pc = pc * inv
    o_cmp = _dot(pc.astype(BF16), vcmp_ref[...])

    W = NSA_WINDOW + Q
    ws = pl.multiple_of(jnp.maximum(q0 - NSA_WINDOW, 0), Q)
    dist = (qpos_r - ws) - lax.broadcasted_iota(jnp.int32, (R, W), 1)
    wmask = pltpu.bitcast(dist, jnp.uint32) < NSA_WINDOW
    s2 = jnp.where(wmask, _dot_nt(qs, kw_ref[pl.ds(ws, W), :]), MASK_VALUE)
    pw = jnp.exp2(s2 - jnp.max(s2, axis=-1, keepdims=True))
    ow = _dot(pw.astype(BF16), vw_ref[pl.ds(ws, W), :])
    o_win = ow[:, 0:HEAD_DIM] / ow[:, HEAD_DIM:]

    p_sum = pc[0:Q] + pc[Q:2 * Q] + pc[2 * Q:3 * Q] + pc[3 * Q:4 * Q]
    ph = p_sum.astype(BF16)
    plo = (p_sum - ph.astype(F32)).astype(BF16)
    imp_t = _dot_nt(ovt_ref[...], ph) + _dot_nt(ovt_ref[...], plo)
    blk = lax.broadcasted_iota(jnp.int32, (nsl, Q), 0)
    blkf = blk.astype(F32)
    qpos_l = q0 + lax.broadcasted_iota(jnp.int32, (nsl, Q), 1)
    cur = jnp.right_shift(qpos_l, NSA_SLC_SHIFT)
    forced = (blk == 0) | (blk == cur) | (blk == cur - 1)
    future = blk * NSA_SLC_LEN > qpos_l
    work = jnp.where(future, -1.0, jnp.where(forced, FORCED_SCORE, imp_t))
    work = jnp.where(blk < n_slc, work, -jnp.inf)
    sel = jnp.zeros((nsl, Q), F32)
    for _ in range(min(NSA_TOPK, n_slc)):
        mx = jnp.max(work, axis=0, keepdims=True)
        first = jnp.min(jnp.where(work == mx, blkf, float(nsl)), axis=0, keepdims=True)
        hit = blkf == first
        sel = jnp.where(hit, 1.0, sel)
        work = jnp.where(hit, -jnp.inf, work)
    bias_t = jnp.where(future, NSA_OFF, jnp.where(sel > 0.0, 0.0, NSA_OFF))
    bias = bias_t.T.astype(BF16)
    q_aug = jnp.concatenate([qs, jnp.concatenate([bias] * N_HEADS, axis=0)], axis=1)

    n_tiles = (q0 + Q + KT - 1) // KT

    def scores(t, width, causal):
        start = pl.multiple_of(t * KT, KT)
        s2 = _dot_nt(q_aug, ks_ref[pl.ds(start, width), :])
        if causal:
            kpos = start + lax.broadcasted_iota(jnp.int32, (R, width), 1)
            s2 = jnp.where(kpos <= qpos_r, s2, NSA_OFF)
        return start, s2

    def max_tile(t, width, causal):
        _, s2 = scores(t, width, causal)
        m = m_ref[...]
        for c in range(width // KT):
            m = jnp.maximum(m, _lanes4(s2[:, c * KT:(c + 1) * KT]))
        m_ref[...] = m

    def sum_tile(t, width, causal):
        start, s2 = scores(t, width, causal)
        pt = jnp.exp2(s2 - jnp.concatenate([m_ref[...]] * (width // LANES), axis=1))
        acc_ref[...] += _dot(pt.astype(BF16), vs_ref[pl.ds(start, width), :])

    def loop(fn):
        n_full = n_tiles - 1

        def body(u, carry):
            fn(2 * u, 2 * KT, False)
            return carry

        lax.fori_loop(0, n_full // 2, body, 0)

        @pl.when(n_full % 2 == 1)
        def _():
            fn(n_full - 1, KT, False)

        fn(n_tiles - 1, KT, True)

    m_ref[...] = jnp.full_like(m_ref, NSA_OFF)
    loop(max_tile)
    m_ref[...] = jnp.broadcast_to(jnp.max(m_ref[...], axis=-1, keepdims=True), m_ref.shape)
    acc_ref[...] = jnp.zeros_like(acc_ref)
    loop(sum_tile)
    o_slc = acc_ref[:, 0:HEAD_DIM] / acc_ref[:, HEAD_DIM:]

    gate = jax.nn.sigmoid(g_ref[...])
    for h in range(N_HEADS):
        rows = slice(h * Q, (h + 1) * Q)
        out = (gate[:, 3 * h:3 * h + 1] * o_cmp[rows] + gate[:, 3 * h + 1:3 * h + 2] * o_slc[rows]
               + gate[:, 3 * h + 2:3 * h + 3] * o_win[rows])
        o_ref[:, h * HEAD_DIM:(h + 1) * HEAD_DIM] = out.astype(o_ref.dtype)


def _nsa_overlap(S):
    n_cmp = (S - NSA_CMP_LEN) // NSA_CMP_STRIDE + 1
    n_slc = S // NSA_SLC_LEN
    nblk = S // NSA_CMP_STRIDE
    nsl = NSA_MAX_SLC
    assert n_slc <= nsl
    cmp_start = np.arange(n_cmp) * NSA_CMP_STRIDE
    slc_start = np.arange(n_slc) * NSA_SLC_LEN
    ov = (np.minimum(cmp_start[:, None] + NSA_CMP_LEN, slc_start[None, :] + NSA_SLC_LEN)
          - np.maximum(cmp_start[:, None], slc_start[None, :]))
    out = np.zeros((nblk, nsl), np.float32)
    out[:n_cmp, :n_slc] = np.clip(ov, 0, None) / NSA_CMP_LEN
    return out, n_slc


def _nsa_attention(qn16, p32, kcmp, vcmp, ks16, vs16, kw16, vw16, casts, layer, S):
    Q = NSA_Q
    steps = S // Q
    ov, n_slc = _nsa_overlap(S)
    nblk, nsl = ov.shape
    slabs = [(w.shape[1] // steps, w.shape[2]) for w in casts]
    full = lambda shape: pl.BlockSpec(shape, lambda i: (0,) * len(shape))
    outs = pl.pallas_call(
        functools.partial(_nsa_attn_kernel, n_slc, len(casts)),
        grid=(steps,),
        in_specs=[pl.BlockSpec((Q, GROUP_WIDTH), lambda i: (i, 0)),
                  pl.BlockSpec((Q, HEAD_DIM), lambda i: (i, COL_NG)),
                  full((nblk, HEAD_DIM)), full((nblk, HEAD_DIM)), full((nsl, nblk)),
                  full((S, HEAD_DIM + NSA_MAX_SLC)), full((S, 2 * HEAD_DIM)),
                  full((S, HEAD_DIM)), full((S, 2 * HEAD_DIM))]
        + [_layer_spec(layer, slab, lambda i: (i, 0)) for slab in slabs],
        out_specs=[pl.BlockSpec((Q, GROUP_WIDTH), lambda i: (i, 0))]
        + [pl.BlockSpec((None,) + slab, lambda i: (0, i, 0)) for slab in slabs],
        out_shape=[jax.ShapeDtypeStruct((S, GROUP_WIDTH), BF16)]
        + [jax.ShapeDtypeStruct((1,) + w.shape[1:], BF16) for w in casts],
        scratch_shapes=[pltpu.VMEM((NSA_ROWS, LANES), F32), pltpu.VMEM((NSA_ROWS, 2 * HEAD_DIM), F32)],
        compiler_params=_cparams(("arbitrary",), 56),
    )(qn16, p32, kcmp, vcmp, jnp.asarray(ov.T, BF16), ks16, vs16, kw16, vw16, *casts)
    return outs[0], outs[1:]


def _outproj_kernel(a0_ref, a1_ref, a2_ref, a3_ref, w_ref, x_ref, o_ref):
    acc = x_ref[...]
    for g, a_ref in enumerate((a0_ref, a1_ref, a2_ref, a3_ref)):
        acc = acc + _dot(a_ref[...], w_ref[g * GROUP_WIDTH:(g + 1) * GROUP_WIDTH, :])
    o_ref[...] = acc


def _outproj(parts, w16, x, tm):
    S, N = x.shape
    a_spec = pl.BlockSpec((tm, GROUP_WIDTH), lambda i: (i, 0))
    return pl.pallas_call(
        _outproj_kernel,
        grid=(S // tm,),
        in_specs=[a_spec, a_spec, a_spec, a_spec,
                  _layer_spec(0, (4 * GROUP_WIDTH, N)),
                  pl.BlockSpec((tm, N), lambda i: (i, 0))],
        out_specs=pl.BlockSpec((tm, N), lambda i: (i, 0)),
        out_shape=jax.ShapeDtypeStruct((S, N), F32),
        compiler_params=_cparams(("parallel",), 48),
    )(*parts, w16, x)


def _xattn_kernel(x_ref, g_ref, wq_ref, k_ref, v_ref, wo_ref, o_ref):
    x = x_ref[...]
    h = _rms(x, g_ref[...]).astype(BF16)
    q = _dot(h, wq_ref[...]).astype(BF16)
    outs = []
    for hd in range(N_HEADS):
        hs = slice(hd * HEAD_DIM, (hd + 1) * HEAD_DIM)
        s = _dot_nt(q[:, hs], k_ref[:, hs]) * ATTN_SCALE
        m = jnp.max(s, axis=-1, keepdims=True)
        p = jnp.exp(s - m)
        p = p / jnp.sum(p, axis=-1, keepdims=True)
        outs.append(_dot(p.astype(BF16), v_ref[:, hs]).astype(BF16))
    o = jnp.concatenate(outs, axis=-1)
    o_ref[...] = x + _dot(o, wo_ref[...])


def _xattn(x, gains, wq16, k16, v16, wo16, layer, tm):
    S, K = x.shape
    XW = wq16.shape[2]
    M = k16.shape[0]
    return pl.pallas_call(
        _xattn_kernel,
        grid=(S // tm,),
        in_specs=[pl.BlockSpec((tm, K), lambda i: (i, 0)),
                  _layer_spec(layer, (1, K)),
                  _layer_spec(0, (K, XW)),
                  pl.BlockSpec((M, XW), lambda i: (0, 0)),
                  pl.BlockSpec((M, XW), lambda i: (0, 0)),
                  _layer_spec(0, (XW, K))],
        out_specs=pl.BlockSpec((tm, K), lambda i: (i, 0)),
        out_shape=jax.ShapeDtypeStruct((S, K), F32),
        compiler_params=_cparams(("parallel",), 48),
    )(x, gains, wq16, k16, v16, wo16)


FFN_HALO = 16
FFN_SUB = 512


def _ffn_kernel(final, x_ref, xp_ref, g_ref, wg_ref, wv_ref, cwg_ref, cwv_ref, cbg_ref, cbv_ref, wd_ref,
                fg_ref, o_ref, h_ref):
    i = pl.program_id(0)
    f = pl.program_id(1)
    tf = wg_ref.shape[1]

    @pl.when(f == 0)
    def _():
        hp = _rms(xp_ref[...], g_ref[...])
        h_ref[0:FFN_HALO, :] = jnp.where(i > 0, hp, 0.0).astype(BF16)
        h_ref[FFN_HALO:, :] = _rms(x_ref[...], g_ref[...]).astype(BF16)
        o_ref[...] = x_ref[...]

    h = h_ref[...]

    def conv(w_ref, cw_ref, cb_ref, cols):
        u = _dot(h, w_ref[:, cols])
        u1 = pltpu.roll(u, 1, axis=0)
        u2 = pltpu.roll(u, 2, axis=0)
        c = cb_ref[:, cols] + cw_ref[2:3, cols] * u + cw_ref[1:2, cols] * u1 + cw_ref[0:1, cols] * u2
        return c[FFN_HALO:, :]

    acts = []
    for s in range(tf // FFN_SUB):
        cols = slice(s * FFN_SUB, (s + 1) * FFN_SUB)
        gate = conv(wg_ref, cwg_ref, cbg_ref, cols)
        val = conv(wv_ref, cwv_ref, cbv_ref, cols)
        acts.append((_silu(gate) * val).astype(BF16))
    o_ref[...] += _dot(jnp.concatenate(acts, axis=1), wd_ref[...])

    if final:
        @pl.when(f == pl.num_programs(1) - 1)
        def _():
            o_ref[...] = _rms(o_ref[...], fg_ref[...])


def _ffn(x, gains, w_up16, conv_w, conv_b, w_down16, final_gain, layer, tm, tf):
    S, K = x.shape
    F = w_down16.shape[1]
    nf = F // tf
    hb = tm // FFN_HALO
    return pl.pallas_call(
        functools.partial(_ffn_kernel, layer == DEPTH - 1),
        grid=(S // tm, nf),
        in_specs=[pl.BlockSpec((tm, K), lambda i, f: (i, 0), pipeline_mode=pl.Buffered(1)),
                  pl.BlockSpec((FFN_HALO, K), lambda i, f: (jnp.maximum(i * hb - 1, 0), 0)),
                  _layer_spec(layer, (1, K)),
                  _layer_spec(0, (K, tf), lambda i, f: (0, f)),
                  _layer_spec(0, (K, tf), lambda i, f: (0, nf + f)),
                  _layer_spec(layer, (3, tf), lambda i, f: (0, f)),
                  _layer_spec(layer, (3, tf), lambda i, f: (0, nf + f)),
                  _layer_spec(layer, (1, tf), lambda i, f: (0, f)),
                  _layer_spec(layer, (1, tf), lambda i, f: (0, nf + f)),
                  _layer_spec(0, (tf, K), lambda i, f: (f, 0)),
                  pl.BlockSpec((1, K), lambda i, f: (0, 0))],
        out_specs=pl.BlockSpec((tm, K), lambda i, f: (i, 0)),
        out_shape=jax.ShapeDtypeStruct((S, K), F32),
        scratch_shapes=[pltpu.VMEM((tm + FFN_HALO, K), BF16)],
        compiler_params=_cparams(("parallel", "arbitrary"), 56),
    )(x, x, gains, w_up16, w_up16, conv_w, conv_w, conv_b, conv_b, w_down16, final_gain)


def kernel(x, mem, positions, mix_norm, w_in, ret_norm, hgrn_lb_logits, hgrn_norm, nsa_pos_k, nsa_pos_v,
           nsa_w_ck, nsa_w_cv, w_out, xattn_norm, mem_norm, xattn_wq, xattn_wk, xattn_wv, xattn_wo,
           ffn_norm, ffn_w_up, ffn_conv_w, ffn_conv_b, ffn_w_down, final_norm):
    B, S, _ = x.shape
    assert B == 1 and S % 1024 == 0
    L = DEPTH
    xs = x.reshape(S, D_MODEL)
    mem2 = mem.reshape(N_MEM, D_MODEL)
    row = lambda g: g.reshape(L, 1, g.shape[-1])
    w_in16 = jnp.pad(w_in.astype(BF16), ((0, 0), (0, 0), (0, IN_COLS_PAD - IN_COLS)))
    wck16 = nsa_w_ck.reshape(L, NSA_CMP_LEN * HEAD_DIM, HEAD_DIM).astype(BF16)
    wcv16 = nsa_w_cv.reshape(L, NSA_CMP_LEN * HEAD_DIM, HEAD_DIM).astype(BF16)
    pos_k = nsa_pos_k.reshape(L, 1, NSA_CMP_LEN * HEAD_DIM)
    pos_v = nsa_pos_v.reshape(L, 1, NSA_CMP_LEN * HEAD_DIM)
    mix_g, ret_g, hgrn_g, xattn_g, mem_g, ffn_g = (
        row(g) for g in (mix_norm, ret_norm, hgrn_norm, xattn_norm, mem_norm, ffn_norm))
    conv_b = row(ffn_conv_b)
    final_g = final_norm.reshape(1, D_MODEL)

    cos, sin = _rope_tables(positions, S)
    for layer in range(L):
        p32, p16 = _inproj(xs, mix_g, w_in16, layer, PROJ_TM, PROJ_TN)
        o_ret = _retention(p32, p16, cos, sin, ret_g, layer, S)
        o_sb = _stick_breaking(p16, S)
        o_hg = _hgrn2(p32, p16, hgrn_lb_logits, hgrn_g, layer, S)
        qn16, kc16, vc16, ks16, kw16, vs16, vw16 = _nsa_prep(p32, p16, cos, sin, S)
        kcmp, vcmp = _nsa_compress(kc16, vc16, wck16, wcv16, pos_k, pos_v, layer, S)
        o_nsa, (w_out16, wq16, wk16, wv16, wo16, w_up16, w_down16) = _nsa_attention(
            qn16, p32, kcmp, vcmp, ks16, vs16, kw16, vw16,
            (w_out, xattn_wq, xattn_wk, xattn_wv, xattn_wo, ffn_w_up, ffn_w_down), layer, S)
        xs = _outproj((o_ret, o_sb, o_hg, o_nsa), w_out16, xs, ROW_TM)
        k16 = _normproj(mem2, mem_g, wk16, layer, BF16)
        v16 = _normproj(mem2, mem_g, wv16, layer, BF16)
        xs = _xattn(xs, xattn_g, wq16, k16, v16, wo16, layer, ROW_TM)
        xs = _ffn(xs, ffn_g, w_up16, ffn_conv_w, conv_b, w_down16, final_g, layer, FFN_TM, FFN_TF)
    return xs.reshape(B, S, D_MODEL)
```

```python
import functools
import math

import numpy as np
import jax
import jax.numpy as jnp
from jax import lax
from jax.experimental import pallas as pl
from jax.experimental.pallas import tpu as pltpu

F32 = jnp.float32
BF16 = jnp.bfloat16

D_MODEL = 2048
DEPTH = 4
N_HEADS = 4
HEAD_DIM = 128
GROUP_WIDTH = N_HEADS * HEAD_DIM
ROPE_THETA = 10000.0
RET_CHUNK = 128
HGRN_CHUNK = 64
NSA_CMP_LEN = 32
NSA_CMP_STRIDE = 16
NSA_SLC_LEN = 64
NSA_SLC_SHIFT = 6
NSA_TOPK = 16
NSA_WINDOW = 512
N_MEM = 256
D_FF = 5632
NORM_EPS = 1e-6
MASK_VALUE = -1e30
FORCED_SCORE = 1e4
MIN_FORGET = 1e-6
IN_COLS = 12 * GROUP_WIDTH + 6 * HEAD_DIM + 3 * N_HEADS
ATTN_SCALE = HEAD_DIM ** -0.5
LANES = 128


IN_COLS_PAD = 7168

COL_RQ, COL_RK, COL_RV, COL_RG = 0, 4, 8, 12
COL_SQ, COL_SK, COL_SV = 16, 20, 24
COL_GQ, COL_GF, COL_GI, COL_GG = 28, 32, 36, 40
COL_NQ = 44
COL_KC, COL_VC, COL_KS, COL_VS, COL_KW, COL_VW, COL_NG = 48, 49, 50, 51, 52, 53, 54

V7X_VMEM_BYTES = 64 * 1024 * 1024

PROJ_TM, PROJ_TN = 1024, 1024
ROW_TM = 512
FFN_TM, FFN_TF = 1024, 512


def _cparams(semantics, vmem_mb):
    return pltpu.CompilerParams(dimension_semantics=semantics,
                                vmem_limit_bytes=min(vmem_mb * 1024 * 1024, V7X_VMEM_BYTES - (8 << 20)))


def _layer_spec(layer, tail_shape, tail_index=None):
    if tail_index is None:
        tail_index = lambda *ids: (0,) * len(tail_shape)
    return pl.BlockSpec((None,) + tuple(tail_shape), lambda *ids: (layer,) + tuple(tail_index(*ids)))


def _dot(a, b):
    return jnp.dot(a, b, preferred_element_type=F32)


def _dot_nt(a, b):
    return lax.dot_general(a, b, (((1,), (1,)), ((), ())), preferred_element_type=F32)


def _dot_tn(a, b):
    return lax.dot_general(a, b, (((0,), (0,)), ((), ())), preferred_element_type=F32)


def _split3(x):
    h1 = x.astype(BF16)
    r1 = x - h1.astype(F32)
    h2 = r1.astype(BF16)
    r2 = r1 - h2.astype(F32)
    return h1, h2, r2.astype(BF16)


def _rms(x, gain):
    ms = jnp.mean(x * x, axis=-1, keepdims=True)
    return x * lax.rsqrt(ms + NORM_EPS) * gain


def _rope(x, cos, sin_signed):
    return x * cos + pltpu.roll(x, HEAD_DIM // 2, axis=1) * sin_signed


def _silu(x):
    return x * jax.nn.sigmoid(x)


def _rope_table_kernel(pos_ref, inv_ref, cos_ref, sin_ref):
    ang = pos_ref[...].astype(F32) * inv_ref[...]
    lane = lax.broadcasted_iota(jnp.int32, ang.shape, 1)
    cos_ref[...] = jnp.cos(ang)
    s = jnp.sin(ang)
    sin_ref[...] = jnp.where(lane < HEAD_DIM // 2, -s, s)


def _rope_tables(positions, S):
    half = HEAD_DIM // 2
    inv = ROPE_THETA ** (-jnp.arange(half, dtype=F32) / half)
    inv = jnp.concatenate([inv, inv]).reshape(1, HEAD_DIM)
    pos = positions.reshape(S, 1)
    tm = min(S, 1024)
    return pl.pallas_call(
        _rope_table_kernel,
        grid=(S // tm,),
        in_specs=[pl.BlockSpec((tm, 1), lambda i: (i, 0)),
                  pl.BlockSpec((1, HEAD_DIM), lambda i: (0, 0))],
        out_specs=[pl.BlockSpec((tm, HEAD_DIM), lambda i: (i, 0)),
                   pl.BlockSpec((tm, HEAD_DIM), lambda i: (i, 0))],
        out_shape=[jax.ShapeDtypeStruct((S, HEAD_DIM), F32)] * 2,
        compiler_params=_cparams(("arbitrary",), 32),
    )(pos, inv)


def _inproj_kernel(x_ref, g_ref, w_ref, o32_ref, o16_ref, h_ref):
    @pl.when(pl.program_id(1) == 0)
    def _():
        h_ref[...] = _rms(x_ref[...], g_ref[...]).astype(BF16)

    y = _dot(h_ref[...], w_ref[...])
    o32_ref[...] = y
    o16_ref[...] = y.astype(BF16)


def _inproj(x, gains, w16, layer, tm, tn):
    S, K = x.shape
    N = w16.shape[2]
    return pl.pallas_call(
        _inproj_kernel,
        grid=(S // tm, N // tn),
        in_specs=[pl.BlockSpec((tm, K), lambda i, j: (i, 0)),
                  _layer_spec(layer, (1, K)),
                  _layer_spec(layer, (K, tn), lambda i, j: (0, j))],
        out_specs=[pl.BlockSpec((tm, tn), lambda i, j: (i, j)),
                   pl.BlockSpec((tm, tn), lambda i, j: (i, j))],
        out_shape=[jax.ShapeDtypeStruct((S, N), F32), jax.ShapeDtypeStruct((S, N), BF16)],
        scratch_shapes=[pltpu.VMEM((tm, K), BF16)],
        compiler_params=_cparams(("parallel", "arbitrary"), 48),
    )(x, gains, w16)


def _normproj_kernel(x_ref, g_ref, w_ref, o_ref):
    h = _rms(x_ref[...], g_ref[...]).astype(BF16)
    o_ref[...] = _dot(h, w_ref[...]).astype(o_ref.dtype)


def _normproj(x, gains, w16, layer, out_dtype):
    M, K = x.shape
    N = w16.shape[2]
    return pl.pallas_call(
        _normproj_kernel,
        grid=(1,),
        in_specs=[pl.BlockSpec((M, K), lambda i: (0, 0)),
                  _layer_spec(layer, (1, K)),
                  _layer_spec(0, (K, N))],
        out_specs=pl.BlockSpec((M, N), lambda i: (0, 0)),
        out_shape=jax.ShapeDtypeStruct((M, N), out_dtype),
        compiler_params=_cparams(("arbitrary",), 32),
    )(x, gains, w16)


_RET_LOG_GAMMA = [math.log1p(-(2.0 ** (-5.0 - h))) for h in range(N_HEADS)]
RET_STEP = 4


def _ret_kernel(q_ref, k_ref, v_ref, g_ref, cos_ref, sin_ref, gain_ref, o_ref, st_ref):
    C = RET_CHUNK

    @pl.when(pl.program_id(0) == 0)
    def _():
        st_ref[...] = jnp.zeros_like(st_ref)

    row = lax.broadcasted_iota(jnp.int32, (C, C), 0)
    col = lax.broadcasted_iota(jnp.int32, (C, C), 1)
    rel = (row - col).astype(F32)
    rowf = row.astype(F32)
    for h in range(N_HEADS):
        hs = slice(h * HEAD_DIM, (h + 1) * HEAD_DIM)
        lg = _RET_LOG_GAMMA[h]
        decay = jnp.where(rel >= 0, jnp.exp(lg * jnp.maximum(rel, 0.0)), 0.0)
        q_w = jnp.exp(lg * (rowf + 1.0))
        k_w = jnp.exp(lg * (C - 1.0 - rowf))
        st = st_ref[h]
        for c in range(RET_STEP):
            rs = slice(c * C, (c + 1) * C)
            cos = cos_ref[rs, :]
            sin = sin_ref[rs, :]
            q = _rope(q_ref[rs, hs], cos, sin)
            k = _rope(k_ref[rs, hs], cos, sin) * ATTN_SCALE
            v = v_ref[rs, hs]
            s = _dot_nt(q.astype(BF16), k.astype(BF16)) * decay
            o = _dot(s.astype(BF16), v) + _dot_nt((q * q_w).astype(BF16), st.astype(BF16))
            st = st * math.exp(lg * C) + _dot_tn(v, (k * k_w).astype(BF16))
            oc = o - jnp.mean(o, axis=-1, keepdims=True)
            y = oc * lax.rsqrt(jnp.mean(oc * oc, axis=-1, keepdims=True) + NORM_EPS) * gain_ref[:, hs]
            o_ref[rs, hs] = (y * _silu(g_ref[rs, hs])).astype(o_ref.dtype)
        st_ref[h] = st


def _retention(p32, p16, cos, sin, gains, layer, S):
    rows = RET_CHUNK * RET_STEP
    gw = GROUP_WIDTH // LANES
    return pl.pallas_call(
        _ret_kernel,
        grid=(S // rows,),
        in_specs=[pl.BlockSpec((rows, GROUP_WIDTH), lambda c: (c, COL_RQ // gw)),
                  pl.BlockSpec((rows, GROUP_WIDTH), lambda c: (c, COL_RK // gw)),
                  pl.BlockSpec((rows, GROUP_WIDTH), lambda c: (c, COL_RV // gw)),
                  pl.BlockSpec((rows, GROUP_WIDTH), lambda c: (c, COL_RG // gw)),
                  pl.BlockSpec((rows, HEAD_DIM), lambda c: (c, 0)),
                  pl.BlockSpec((rows, HEAD_DIM), lambda c: (c, 0)),
                  _layer_spec(layer, (1, GROUP_WIDTH))],
        out_specs=pl.BlockSpec((rows, GROUP_WIDTH), lambda c: (c, 0)),
        out_shape=jax.ShapeDtypeStruct((S, GROUP_WIDTH), BF16),
        scratch_shapes=[pltpu.VMEM((N_HEADS, HEAD_DIM, HEAD_DIM), F32)],
        compiler_params=_cparams(("arbitrary",), 32),
    )(p32, p32, p16, p32, cos, sin, gains)


SB_TILE = 256
SB_HEADS = 4
SB_DEAD_LOG = -104.0


def _sb_tile(q, k_ref, v_ref, hs, j, run, acc, upper, diag, on=None):
    T = SB_TILE
    start = pl.multiple_of(j * T, T)
    k = k_ref[pl.ds(start, T), hs]
    v = v_ref[pl.ds(start, T), hs]
    z = _dot_nt(q, k) * ATTN_SCALE
    sp = jnp.maximum(z, 0.0) + jnp.log(1.0 + jnp.exp(-jnp.abs(z)))
    log_1m = -sp
    strict = on
    if diag:
        row = lax.broadcasted_iota(jnp.int32, (T, T), 0)
        col = lax.broadcasted_iota(jnp.int32, (T, T), 1)
        strict = col < row
    if strict is not None:
        log_1m = jnp.where(strict, log_1m, 0.0)
    hi = log_1m.astype(BF16)
    lo = (log_1m - hi.astype(F32)).astype(BF16)
    both = _dot(jnp.concatenate([hi, lo], axis=0), upper)
    between = both[:T] + both[T:] + run
    w = jnp.exp((z - sp) + between)
    if strict is not None:
        w = jnp.where(strict, w, 0.0)
    acc = acc + _dot(w.astype(BF16), v)
    run = between[:, 0:1] + log_1m[:, 0:1]
    return run, acc


def _sb_kernel(q_ref, k_ref, v_ref, o_ref):
    T = SB_TILE
    i = pl.program_id(1)
    r = lax.broadcasted_iota(jnp.int32, (T, T), 0)
    c = lax.broadcasted_iota(jnp.int32, (T, T), 1)
    upper = jnp.where(r > c, 1.0, 0.0).astype(BF16)

    def alive(run):
        return (jnp.max(run) > SB_DEAD_LOG).astype(jnp.int32)

    heads = []
    for h in range(SB_HEADS):
        hs = slice(h * HEAD_DIM, (h + 1) * HEAD_DIM)
        q = q_ref[:, hs]
        run = jnp.zeros((T, 1), F32)
        acc = jnp.zeros((T, HEAD_DIM), F32)
        run, acc = _sb_tile(q, k_ref, v_ref, hs, i, run, acc, upper, True)
        run, acc = _sb_tile(q, k_ref, v_ref, hs, jnp.maximum(i - 1, 0), run, acc, upper, False, on=i > 0)
        heads.append((hs, q, run, acc))

    for hs, q, run, acc in heads:
        def cond(carry):
            step, live, _, _ = carry
            return jnp.logical_and(step < i, live > 0)

        def body(carry, hs=hs, q=q):
            step, _, run, acc = carry
            run, acc = _sb_tile(q, k_ref, v_ref, hs, i - 1 - step, run, acc, upper, False)
            return step + 1, alive(run), run, acc

        _, _, _, acc = lax.while_loop(cond, body, (jnp.int32(1), alive(run), run, acc))
        o_ref[:, hs] = acc.astype(o_ref.dtype)


def _stick_breaking(p16, S):
    T = SB_TILE
    W = SB_HEADS * HEAD_DIM
    return pl.pallas_call(
        _sb_kernel,
        grid=(N_HEADS // SB_HEADS, S // T),
        in_specs=[pl.BlockSpec((T, W), lambda h, i: (i, COL_SQ // SB_HEADS + h)),
                  pl.BlockSpec((S, W), lambda h, i: (0, COL_SK // SB_HEADS + h)),
                  pl.BlockSpec((S, W), lambda h, i: (0, COL_SV // SB_HEADS + h))],
        out_specs=pl.BlockSpec((T, W), lambda h, i: (i, h)),
        out_shape=jax.ShapeDtypeStruct((S, GROUP_WIDTH), BF16),
        compiler_params=_cparams(("parallel", "arbitrary"), 40),
    )(p16, p16, p16)


_HG_LEVELS = (32, 16, 8, 4, 2, 1)
HGRN_STEP = 4


def _hgrn_constants():
    C = HGRN_CHUNK
    n = np.arange(C)
    seg = np.zeros((8 * C, C), np.float32)
    pair = np.zeros((7, C, C), np.float32)
    for l, s in enumerate(_HG_LEVELS):
        blk = n // s
        odd = (blk % 2) == 1
        a = blk * s
        e = a + s - 1
        j = n[None, :]
        t_odd = (j >= a[:, None]) & (j <= n[:, None])
        t_even = (j > n[:, None]) & (j <= e[:, None])
        seg[l * C:(l + 1) * C] = np.where(odd[:, None], t_odd, t_even)
        pair[l] = odd[:, None] & (blk[None, :] == blk[:, None] - 1)
    seg[6 * C:7 * C] = n[None, :] <= n[:, None]
    seg[7 * C:8 * C] = n[None, :] > n[:, None]
    pair[6] = np.eye(C)
    return seg, pair


def _dot_nt_split(a, b):
    a_hi = a.astype(BF16)
    a_lo = (a - a_hi.astype(F32)).astype(BF16)
    b_hi = b.astype(BF16)
    b_lo = (b - b_hi.astype(F32)).astype(BF16)
    return _dot_nt(jnp.concatenate([a_hi, a_lo, a_hi], axis=1), jnp.concatenate([b_hi, b_hi, b_lo], axis=1))


def _hgrn_kernel(layer, q_ref, f_ref, v_ref, g_ref, lbl_ref, gain_ref, seg_ref, pair_ref, o_ref, st_ref):
    C = HGRN_CHUNK

    @pl.when(pl.program_id(0) == 0)
    def _():
        st_ref[...] = jnp.zeros_like(st_ref)

    logits = lbl_ref[...]
    e = jnp.exp(logits - jnp.max(logits, axis=0, keepdims=True))
    p = e / jnp.sum(e, axis=0, keepdims=True)
    lb = jnp.zeros((1, GROUP_WIDTH), F32)
    for l in range(1, layer + 1):
        lb = lb + p[l:l + 1, :]

    f = lb + (1.0 - lb) * jax.nn.sigmoid(f_ref[...])
    log_f = jnp.log(jnp.maximum(f, MIN_FORGET))
    kk = 1.0 - f
    parts = _split3(log_f)
    zero = jnp.zeros((C, GROUP_WIDTH), BF16)
    rhs = jnp.concatenate(
        [jnp.concatenate([p[c * C:(c + 1) * C] for p in parts] + [zero], axis=0) for c in range(HGRN_STEP)],
        axis=1)
    ex = jnp.exp(_dot(seg_ref[...], rhs))

    for h in range(N_HEADS):
        hs = slice(h * HEAD_DIM, (h + 1) * HEAD_DIM)
        st = st_ref[h]
        for c in range(HGRN_STEP):
            rs = slice(c * C, (c + 1) * C)
            es = slice(c * GROUP_WIDTH + h * HEAD_DIM, c * GROUP_WIDTH + (h + 1) * HEAD_DIM)
            q = q_ref[rs, hs] * ATTN_SCALE
            k = kk[rs, hs]
            v = v_ref[rs, hs]
            s = pair_ref[6] * _dot_nt_split(q, k)
            for l in range(6):
                el = ex[l * C:(l + 1) * C, es]
                s = s + pair_ref[l] * _dot_nt_split(q * el, k * el)
            e_cum = ex[6 * C:7 * C, es]
            e_rest = ex[7 * C:8 * C, es]
            o = _dot(s.astype(BF16), v) + _dot_nt((q * e_cum).astype(BF16), st.astype(BF16))
            st = st * e_cum[C - 1:C, :] + _dot_tn(v, (k * e_rest).astype(BF16))
            y = o * lax.rsqrt(jnp.mean(o * o, axis=-1, keepdims=True) + NORM_EPS) * gain_ref[:, hs]
            o_ref[rs, hs] = (y * _silu(g_ref[rs, hs])).astype(o_ref.dtype)
        st_ref[h] = st


def _hgrn2(p32, p16, lb_logits, gains, layer, S):
    C = HGRN_CHUNK
    rows = C * HGRN_STEP
    gw = GROUP_WIDTH // LANES
    seg, pair = _hgrn_constants()
    seg = np.concatenate([seg, seg, seg, np.zeros_like(seg)], axis=1)
    return pl.pallas_call(
        functools.partial(_hgrn_kernel, layer),
        grid=(S // rows,),
        in_specs=[pl.BlockSpec((rows, GROUP_WIDTH), lambda c: (c, COL_GQ // gw)),
                  pl.BlockSpec((rows, GROUP_WIDTH), lambda c: (c, COL_GF // gw)),
                  pl.BlockSpec((rows, GROUP_WIDTH), lambda c: (c, COL_GI // gw)),
                  pl.BlockSpec((rows, GROUP_WIDTH), lambda c: (c, COL_GG // gw)),
                  pl.BlockSpec((DEPTH, GROUP_WIDTH), lambda c: (0, 0)),
                  _layer_spec(layer, (1, GROUP_WIDTH)),
                  pl.BlockSpec((8 * C, 4 * C), lambda c: (0, 0)),
                  pl.BlockSpec((7, C, C), lambda c: (0, 0, 0))],
        out_specs=pl.BlockSpec((rows, GROUP_WIDTH), lambda c: (c, 0)),
        out_shape=jax.ShapeDtypeStruct((S, GROUP_WIDTH), BF16),
        scratch_shapes=[pltpu.VMEM((N_HEADS, HEAD_DIM, HEAD_DIM), F32)],
        compiler_params=_cparams(("arbitrary",), 32),
    )(p32, p32, p16, p32, lb_logits, gains,
      jnp.asarray(seg, BF16), jnp.asarray(pair, F32))


NSA_Q_SCALE = ATTN_SCALE * math.log2(math.e)
NSA_MAX_SLC = LANES


def _nsa_prep_kernel(q_ref, kc_ref, vc_ref, ks_ref, kw_ref, vs_ref, vw_ref, cos_ref, sin_ref,
                     qo_ref, kco_ref, vco_ref, kso_ref, kwo_ref, vso_ref, vwo_ref):
    tm = q_ref.shape[0]
    ones = jnp.ones((tm, HEAD_DIM), BF16)
    vso_ref[:, 0:HEAD_DIM] = vs_ref[...]
    vso_ref[:, HEAD_DIM:] = ones
    vwo_ref[:, 0:HEAD_DIM] = vw_ref[...]
    vwo_ref[:, HEAD_DIM:] = ones
    cos = cos_ref[...]
    sin = sin_ref[...]
    for h in range(N_HEADS):
        hs = slice(h * HEAD_DIM, (h + 1) * HEAD_DIM)
        qo_ref[:, hs] = (_rope(q_ref[:, hs], cos, sin) * NSA_Q_SCALE).astype(BF16)
    kco_ref[...] = _rope(kc_ref[...], cos, sin).astype(BF16)
    kwo_ref[...] = _rope(kw_ref[...], cos, sin).astype(BF16)
    vco_ref[...] = vc_ref[...]
    kso_ref[:, 0:HEAD_DIM] = _rope(ks_ref[...], cos, sin).astype(BF16)
    tok = pl.program_id(0) * tm + lax.broadcasted_iota(jnp.int32, (tm, NSA_MAX_SLC), 0)
    lane = lax.broadcasted_iota(jnp.int32, (tm, NSA_MAX_SLC), 1)
    kso_ref[:, HEAD_DIM:] = jnp.where(jnp.right_shift(tok, NSA_SLC_SHIFT) == lane, 1.0, 0.0).astype(BF16)


def _nsa_prep(p32, p16, cos, sin, S):
    tm = min(S, 512)
    gw = GROUP_WIDTH // LANES
    row = lambda col: pl.BlockSpec((tm, HEAD_DIM), lambda i: (i, col))
    out = pl.BlockSpec((tm, HEAD_DIM), lambda i: (i, 0))
    aug = pl.BlockSpec((tm, HEAD_DIM + NSA_MAX_SLC), lambda i: (i, 0))
    return pl.pallas_call(
        _nsa_prep_kernel,
        grid=(S // tm,),
        in_specs=[pl.BlockSpec((tm, GROUP_WIDTH), lambda i: (i, COL_NQ // gw)),
                  row(COL_KC), row(COL_VC), row(COL_KS), row(COL_KW), row(COL_VS), row(COL_VW), out, out],
        out_specs=[pl.BlockSpec((tm, GROUP_WIDTH), lambda i: (i, 0)), out, out, aug, out, aug, aug],
        out_shape=[jax.ShapeDtypeStruct((S, GROUP_WIDTH), BF16),
                   jax.ShapeDtypeStruct((S, HEAD_DIM), BF16),
                   jax.ShapeDtypeStruct((S, HEAD_DIM), BF16),
                   jax.ShapeDtypeStruct((S, HEAD_DIM + NSA_MAX_SLC), BF16),
                   jax.ShapeDtypeStruct((S, HEAD_DIM), BF16),
                   jax.ShapeDtypeStruct((S, 2 * HEAD_DIM), BF16),
                   jax.ShapeDtypeStruct((S, 2 * HEAD_DIM), BF16)],
        compiler_params=_cparams(("parallel",), 32),
    )(p32, p32, p16, p32, p32, p16, p16, cos, sin)


def _nsa_compress_kernel(kb_ref, vb_ref, wk_ref, wv_ref, pk_ref, pv_ref, ko_ref, vo_ref):
    nblk = kb_ref.shape[0]
    half = NSA_CMP_STRIDE * HEAD_DIM
    rowid = lax.broadcasted_iota(jnp.int32, (nblk, HEAD_DIM), 0)
    for b_ref, w_ref, p_ref, o_ref in ((kb_ref, wk_ref, pk_ref, ko_ref), (vb_ref, wv_ref, pv_ref, vo_ref)):
        w = w_ref[...]
        top = _dot(b_ref[...], w[:half])
        bot = _dot(b_ref[...], w[half:])
        ph, pm, plo = _split3(p_ref[...])
        const = _dot(ph, w) + _dot(pm, w) + _dot(plo, w)
        shifted = pltpu.roll(bot, nblk - 1, axis=0)
        out = top + shifted + const
        o_ref[...] = jnp.where(rowid < nblk - 1, out, 0.0).astype(BF16)


def _nsa_compress(kc16, vc16, wk16, wv16, pos_k, pos_v, layer, S):
    nblk = S // NSA_CMP_STRIDE
    width = NSA_CMP_STRIDE * HEAD_DIM
    kb = kc16.reshape(nblk, width)
    vb = vc16.reshape(nblk, width)
    full = lambda shape: pl.BlockSpec(shape, lambda i: (0,) * len(shape))
    return pl.pallas_call(
        _nsa_compress_kernel,
        grid=(1,),
        in_specs=[full((nblk, width)), full((nblk, width)),
                  _layer_spec(layer, (2 * width, HEAD_DIM)), _layer_spec(layer, (2 * width, HEAD_DIM)),
                  _layer_spec(layer, (1, 2 * width)), _layer_spec(layer, (1, 2 * width))],
        out_specs=[full((nblk, HEAD_DIM)), full((nblk, HEAD_DIM))],
        out_shape=[jax.ShapeDtypeStruct((nblk, HEAD_DIM), BF16)] * 2,
        compiler_params=_cparams(("arbitrary",), 32),
    )(kb, vb, wk16, wv16, pos_k, pos_v)


NSA_Q = 256
NSA_KT = 512
NSA_ROWS = N_HEADS * NSA_Q
NSA_OFF = -(2.0 ** 126)


def _lanes4(x):
    return jnp.maximum(jnp.maximum(x[:, 0:LANES], x[:, LANES:2 * LANES]),
                       jnp.maximum(x[:, 2 * LANES:3 * LANES], x[:, 3 * LANES:4 * LANES]))


def _nsa_attn_kernel(n_slc, n_cast, q_ref, g_ref, kcmp_ref, vcmp_ref, ovt_ref, ks_ref, vs_ref, kw_ref, vw_ref,
                     *refs):
    cast_in, o_ref, cast_out = refs[:n_cast], refs[n_cast], refs[n_cast + 1:2 * n_cast + 1]
    m_ref, acc_ref = refs[2 * n_cast + 1:]
    Q, KT, R = NSA_Q, NSA_KT, NSA_ROWS
    nsl = NSA_MAX_SLC
    bi = pl.program_id(0)
    for src, dst in zip(cast_in, cast_out):
        dst[...] = src[...].astype(BF16)
    q0 = bi * Q
    ncmp = kcmp_ref.shape[0]
    qs = jnp.concatenate([q_ref[:, h * HEAD_DIM:(h + 1) * HEAD_DIM] for h in range(N_HEADS)], axis=0)
    qpos_r = q0 + jnp.bitwise_and(lax.broadcasted_iota(jnp.int32, (R, 1), 0), Q - 1)

    last_vis = jnp.minimum(jnp.right_shift(qpos_r - (NSA_CMP_LEN - 1), 4), ncmp - 2)
    cmp_mask = lax.broadcasted_iota(jnp.int32, (R, ncmp), 1) <= last_vis
    s2 = jnp.where(cmp_mask, _dot_nt(qs, kcmp_ref[...]), MASK_VALUE)
    pc = jnp.exp2(s2 - jnp.max(s2, axis=-1, keepdims=True))
    inv = jnp.where(last_vis >= 0, 1.0 / jnp.sum(pc, axis=-1, keepdims=True), 0.0)
    pc = pc * inv
    o_cmp = _dot(pc.astype(BF16), vcmp_ref[...])

    W = NSA_WINDOW + Q
    ws = pl.multiple_of(jnp.maximum(q0 - NSA_WINDOW, 0), Q)
    dist = (qpos_r - ws) - lax.broadcasted_iota(jnp.int32, (R, W), 1)
    wmask = pltpu.bitcast(dist, jnp.uint32) < NSA_WINDOW
    s2 = jnp.where(wmask, _dot_nt(qs, kw_ref[pl.ds(ws, W), :]), MASK_VALUE)
    pw = jnp.exp2(s2 - jnp.max(s2, axis=-1, keepdims=True))
    ow = _dot(pw.astype(BF16), vw_ref[pl.ds(ws, W), :])
    o_win = ow[:, 0:HEAD_DIM] / ow[:, HEAD_DIM:]

    p_sum = pc[0:Q] + pc[Q:2 * Q] + pc[2 * Q:3 * Q] + pc[3 * Q:4 * Q]
    ph = p_sum.astype(BF16)
    plo = (p_sum - ph.astype(F32)).astype(BF16)
    imp_t = _dot_nt(ovt_ref[...], ph) + _dot_nt(ovt_ref[...], plo)
    blk = lax.broadcasted_iota(jnp.int32, (nsl, Q), 0)
    blkf = blk.astype(F32)
    qpos_l = q0 + lax.broadcasted_iota(jnp.int32, (nsl, Q), 1)
    cur = jnp.right_shift(qpos_l, NSA_SLC_SHIFT)
    forced = (blk == 0) | (blk == cur) | (blk == cur - 1)
    future = blk * NSA_SLC_LEN > qpos_l
    work = jnp.where(future, -1.0, jnp.where(forced, FORCED_SCORE, imp_t))
    work = jnp.where(blk < n_slc, work, -jnp.inf)
    sel = jnp.zeros((nsl, Q), F32)
    for _ in range(min(NSA_TOPK, n_slc)):
        mx = jnp.max(work, axis=0, keepdims=True)
        first = jnp.min(jnp.where(work == mx, blkf, float(nsl)), axis=0, keepdims=True)
        hit = blkf == first
        sel = jnp.where(hit, 1.0, sel)
        work = jnp.where(hit, -jnp.inf, work)
    bias_t = jnp.where(future, NSA_OFF, jnp.where(sel > 0.0, 0.0, NSA_OFF))
    bias = bias_t.T.astype(BF16)
    q_aug = jnp.concatenate([qs, jnp.concatenate([bias] * N_HEADS, axis=0)], axis=1)

    n_tiles = (q0 + Q + KT - 1) // KT

    def scores(t, width, causal):
        start = pl.multiple_of(t * KT, KT)
        s2 = _dot_nt(q_aug, ks_ref[pl.ds(start, width), :])
        if causal:
            kpos = start + lax.broadcasted_iota(jnp.int32, (R, width), 1)
            s2 = jnp.where(kpos <= qpos_r, s2, NSA_OFF)
        return start, s2

    def max_tile(t, width, causal):
        _, s2 = scores(t, width, causal)
        m = m_ref[...]
        for c in range(width // KT):
            m = jnp.maximum(m, _lanes4(s2[:, c * KT:(c + 1) * KT]))
        m_ref[...] = m

    def sum_tile(t, width, causal):
        start, s2 = scores(t, width, causal)
        pt = jnp.exp2(s2 - jnp.concatenate([m_ref[...]] * (width // LANES), axis=1))
        acc_ref[...] += _dot(pt.astype(BF16), vs_ref[pl.ds(start, width), :])

    def loop(fn):
        n_full = n_tiles - 1

        def body(u, carry):
            fn(2 * u, 2 * KT, False)
            return carry

        lax.fori_loop(0, n_full // 2, body, 0)

        @pl.when(n_full % 2 == 1)
        def _():
            fn(n_full - 1, KT, False)

        fn(n_tiles - 1, KT, True)

    m_ref[...] = jnp.full_like(m_ref, NSA_OFF)
    loop(max_tile)
    m_ref[...] = jnp.broadcast_to(jnp.max(m_ref[...], axis=-1, keepdims=True), m_ref.shape)
    acc_ref[...] = jnp.zeros_like(acc_ref)
    loop(sum_tile)
    o_slc = acc_ref[:, 0:HEAD_DIM] / acc_ref[:, HEAD_DIM:]

    gate = jax.nn.sigmoid(g_ref[...])
    for h in range(N_HEADS):
        rows = slice(h * Q, (h + 1) * Q)
        out = (gate[:, 3 * h:3 * h + 1] * o_cmp[rows] + gate[:, 3 * h + 1:3 * h + 2] * o_slc[rows]
               + gate[:, 3 * h + 2:3 * h + 3] * o_win[rows])
        o_ref[:, h * HEAD_DIM:(h + 1) * HEAD_DIM] = out.astype(o_ref.dtype)


def _nsa_overlap(S):
    n_cmp = (S - NSA_CMP_LEN) // NSA_CMP_STRIDE + 1
    n_slc = S // NSA_SLC_LEN
    nblk = S // NSA_CMP_STRIDE
    nsl = NSA_MAX_SLC
    assert n_slc <= nsl
    cmp_start = np.arange(n_cmp) * NSA_CMP_STRIDE
    slc_start = np.arange(n_slc) * NSA_SLC_LEN
    ov = (np.minimum(cmp_start[:, None] + NSA_CMP_LEN, slc_start[None, :] + NSA_SLC_LEN)
          - np.maximum(cmp_start[:, None], slc_start[None, :]))
    out = np.zeros((nblk, nsl), np.float32)
    out[:n_cmp, :n_slc] = np.clip(ov, 0, None) / NSA_CMP_LEN
    return out, n_slc


def _nsa_attention(qn16, p32, kcmp, vcmp, ks16, vs16, kw16, vw16, casts, layer, S):
    Q = NSA_Q
    steps = S // Q
    ov, n_slc = _nsa_overlap(S)
    nblk, nsl = ov.shape
    slabs = [(w.shape[1] // steps, w.shape[2]) for w in casts]
    full = lambda shape: pl.BlockSpec(shape, lambda i: (0,) * len(shape))
    outs = pl.pallas_call(
        functools.partial(_nsa_attn_kernel, n_slc, len(casts)),
        grid=(steps,),
        in_specs=[pl.BlockSpec((Q, GROUP_WIDTH), lambda i: (i, 0)),
                  pl.BlockSpec((Q, HEAD_DIM), lambda i: (i, COL_NG)),
                  full((nblk, HEAD_DIM)), full((nblk, HEAD_DIM)), full((nsl, nblk)),
                  full((S, HEAD_DIM + NSA_MAX_SLC)), full((S, 2 * HEAD_DIM)),
                  full((S, HEAD_DIM)), full((S, 2 * HEAD_DIM))]
        + [_layer_spec(layer, slab, lambda i: (i, 0)) for slab in slabs],
        out_specs=[pl.BlockSpec((Q, GROUP_WIDTH), lambda i: (i, 0))]
        + [pl.BlockSpec((None,) + slab, lambda i: (0, i, 0)) for slab in slabs],
        out_shape=[jax.ShapeDtypeStruct((S, GROUP_WIDTH), BF16)]
        + [jax.ShapeDtypeStruct((1,) + w.shape[1:], BF16) for w in casts],
        scratch_shapes=[pltpu.VMEM((NSA_ROWS, LANES), F32), pltpu.VMEM((NSA_ROWS, 2 * HEAD_DIM), F32)],
        compiler_params=_cparams(("arbitrary",), 56),
    )(qn16, p32, kcmp, vcmp, jnp.asarray(ov.T, BF16), ks16, vs16, kw16, vw16, *casts)
    return outs[0], outs[1:]


def _outproj_kernel(a0_ref, a1_ref, a2_ref, a3_ref, w_ref, x_ref, o_ref):
    acc = x_ref[...]
    for g, a_ref in enumerate((a0_ref, a1_ref, a2_ref, a3_ref)):
        acc = acc + _dot(a_ref[...], w_ref[g * GROUP_WIDTH:(g + 1) * GROUP_WIDTH, :])
    o_ref[...] = acc


def _outproj(parts, w16, x, tm):
    S, N = x.shape
    a_spec = pl.BlockSpec((tm, GROUP_WIDTH), lambda i: (i, 0))
    return pl.pallas_call(
        _outproj_kernel,
        grid=(S // tm,),
        in_specs=[a_spec, a_spec, a_spec, a_spec,
                  _layer_spec(0, (4 * GROUP_WIDTH, N)),
                  pl.BlockSpec((tm, N), lambda i: (i, 0))],
        out_specs=pl.BlockSpec((tm, N), lambda i: (i, 0)),
        out_shape=jax.ShapeDtypeStruct((S, N), F32),
        compiler_params=_cparams(("parallel",), 48),
    )(*parts, w16, x)


def _xattn_kernel(x_ref, g_ref, wq_ref, k_ref, v_ref, wo_ref, o_ref):
    x = x_ref[...]
    h = _rms(x, g_ref[...]).astype(BF16)
    q = _dot(h, wq_ref[...]).astype(BF16)
    outs = []
    for hd in range(N_HEADS):
        hs = slice(hd * HEAD_DIM, (hd + 1) * HEAD_DIM)
        s = _dot_nt(q[:, hs], k_ref[:, hs]) * ATTN_SCALE
        m = jnp.max(s, axis=-1, keepdims=True)
        p = jnp.exp(s - m)
        p = p / jnp.sum(p, axis=-1, keepdims=True)
        outs.append(_dot(p.astype(BF16), v_ref[:, hs]).astype(BF16))
    o = jnp.concatenate(outs, axis=-1)
    o_ref[...] = x + _dot(o, wo_ref[...])


def _xattn(x, gains, wq16, k16, v16, wo16, layer, tm):
    S, K = x.shape
    XW = wq16.shape[2]
    M = k16.shape[0]
    return pl.pallas_call(
        _xattn_kernel,
        grid=(S // tm,),
        in_specs=[pl.BlockSpec((tm, K), lambda i: (i, 0)),
                  _layer_spec(layer, (1, K)),
                  _layer_spec(0, (K, XW)),
                  pl.BlockSpec((M, XW), lambda i: (0, 0)),
                  pl.BlockSpec((M, XW), lambda i: (0, 0)),
                  _layer_spec(0, (XW, K))],
        out_specs=pl.BlockSpec((tm, K), lambda i: (i, 0)),
        out_shape=jax.ShapeDtypeStruct((S, K), F32),
        compiler_params=_cparams(("parallel",), 48),
    )(x, gains, wq16, k16, v16, wo16)


FFN_HALO = 16
FFN_SUB = 512


def _ffn_kernel(final, x_ref, xp_ref, g_ref, wg_ref, wv_ref, cwg_ref, cwv_ref, cbg_ref, cbv_ref, wd_ref,
                fg_ref, o_ref, h_ref):
    i = pl.program_id(0)
    f = pl.program_id(1)
    tf = wg_ref.shape[1]

    @pl.when(f == 0)
    def _():
        hp = _rms(xp_ref[...], g_ref[...])
        h_ref[0:FFN_HALO, :] = jnp.where(i > 0, hp, 0.0).astype(BF16)
        h_ref[FFN_HALO:, :] = _rms(x_ref[...], g_ref[...]).astype(BF16)
        o_ref[...] = x_ref[...]

    h = h_ref[...]

    def conv(w_ref, cw_ref, cb_ref, cols):
        u = _dot(h, w_ref[:, cols])
        u1 = pltpu.roll(u, 1, axis=0)
        u2 = pltpu.roll(u, 2, axis=0)
        c = cb_ref[:, cols] + cw_ref[2:3, cols] * u + cw_ref[1:2, cols] * u1 + cw_ref[0:1, cols] * u2
        return c[FFN_HALO:, :]

    acts = []
    for s in range(tf // FFN_SUB):
        cols = slice(s * FFN_SUB, (s + 1) * FFN_SUB)
        gate = conv(wg_ref, cwg_ref, cbg_ref, cols)
        val = conv(wv_ref, cwv_ref, cbv_ref, cols)
        acts.append((_silu(gate) * val).astype(BF16))
    o_ref[...] += _dot(jnp.concatenate(acts, axis=1), wd_ref[...])

    if final:
        @pl.when(f == pl.num_programs(1) - 1)
        def _():
            o_ref[...] = _rms(o_ref[...], fg_ref[...])


def _ffn(x, gains, w_up16, conv_w, conv_b, w_down16, final_gain, layer, tm, tf):
    S, K = x.shape
    F = w_down16.shape[1]
    nf = F // tf
    hb = tm // FFN_HALO
    return pl.pallas_call(
        functools.partial(_ffn_kernel, layer == DEPTH - 1),
        grid=(S // tm, nf),
        in_specs=[pl.BlockSpec((tm, K), lambda i, f: (i, 0), pipeline_mode=pl.Buffered(1)),
                  pl.BlockSpec((FFN_HALO, K), lambda i, f: (jnp.maximum(i * hb - 1, 0), 0)),
                  _layer_spec(layer, (1, K)),
                  _layer_spec(0, (K, tf), lambda i, f: (0, f)),
                  _layer_spec(0, (K, tf), lambda i, f: (0, nf + f)),
                  _layer_spec(layer, (3, tf), lambda i, f: (0, f)),
                  _layer_spec(layer, (3, tf), lambda i, f: (0, nf + f)),
                  _layer_spec(layer, (1, tf), lambda i, f: (0, f)),
                  _layer_spec(layer, (1, tf), lambda i, f: (0, nf + f)),
                  _layer_spec(0, (tf, K), lambda i, f: (f, 0)),
                  pl.BlockSpec((1, K), lambda i, f: (0, 0))],
        out_specs=pl.BlockSpec((tm, K), lambda i, f: (i, 0)),
        out_shape=jax.ShapeDtypeStruct((S, K), F32),
        scratch_shapes=[pltpu.VMEM((tm + FFN_HALO, K), BF16)],
        compiler_params=_cparams(("parallel", "arbitrary"), 56),
    )(x, x, gains, w_up16, w_up16, conv_w, conv_w, conv_b, conv_b, w_down16, final_gain)


def kernel(x, mem, positions, mix_norm, w_in, ret_norm, hgrn_lb_logits, hgrn_norm, nsa_pos_k, nsa_pos_v,
           nsa_w_ck, nsa_w_cv, w_out, xattn_norm, mem_norm, xattn_wq, xattn_wk, xattn_wv, xattn_wo,
           ffn_norm, ffn_w_up, ffn_conv_w, ffn_conv_b, ffn_w_down, final_norm):
    B, S, _ = x.shape
    assert B == 1 and S % 1024 == 0
    L = DEPTH
    xs = x.reshape(S, D_MODEL)
    mem2 = mem.reshape(N_MEM, D_MODEL)
    row = lambda g: g.reshape(L, 1, g.shape[-1])
    w_in16 = jnp.pad(w_in.astype(BF16), ((0, 0), (0, 0), (0, IN_COLS_PAD - IN_COLS)))
    wck16 = nsa_w_ck.reshape(L, NSA_CMP_LEN * HEAD_DIM, HEAD_DIM).astype(BF16)
    wcv16 = nsa_w_cv.reshape(L, NSA_CMP_LEN * HEAD_DIM, HEAD_DIM).astype(BF16)
    pos_k = nsa_pos_k.reshape(L, 1, NSA_CMP_LEN * HEAD_DIM)
    pos_v = nsa_pos_v.reshape(L, 1, NSA_CMP_LEN * HEAD_DIM)
    mix_g, ret_g, hgrn_g, xattn_g, mem_g, ffn_g = (
        row(g) for g in (mix_norm, ret_norm, hgrn_norm, xattn_norm, mem_norm, ffn_norm))
    conv_b = row(ffn_conv_b)
    final_g = final_norm.reshape(1, D_MODEL)

    cos, sin = _rope_tables(positions, S)
    for layer in range(L):
        p32, p16 = _inproj(xs, mix_g, w_in16, layer, PROJ_TM, PROJ_TN)
        o_ret = _retention(p32, p16, cos, sin, ret_g, layer, S)
        o_sb = _stick_breaking(p16, S)
        o_hg = _hgrn2(p32, p16, hgrn_lb_logits, hgrn_g, layer, S)
        qn16, kc16, vc16, ks16, kw16, vs16, vw16 = _nsa_prep(p32, p16, cos, sin, S)
        kcmp, vcmp = _nsa_compress(kc16, vc16, wck16, wcv16, pos_k, pos_v, layer, S)
        o_nsa, (w_out16, wq16, wk16, wv16, wo16, w_up16, w_down16) = _nsa_attention(
            qn16, p32, kcmp, vcmp, ks16, vs16, kw16, vw16,
            (w_out, xattn_wq, xattn_wk, xattn_wv, xattn_wo, ffn_w_up, ffn_w_down), layer, S)
        xs = _outproj((o_ret, o_sb, o_hg, o_nsa), w_out16, xs, ROW_TM)
        k16 = _normproj(mem2, mem_g, wk16, layer, BF16)
        v16 = _normproj(mem2, mem_g, wv16, layer, BF16)
        xs = _xattn(xs, xattn_g, wq16, k16, v16, wo16, layer, ROW_TM)
        xs = _ffn(xs, ffn_g, w_up16, ffn_conv_w, conv_b, w_down16, final_g, layer, FFN_TM, FFN_TF)
    return xs.reshape(B, S, D_MODEL)
```
